```python
import math
import jax, jax.numpy as jnp
from jax import lax
import numpy as np

D_MODEL = 1024
BATCH = 2
SEQ = 8192
DEPTH = 4
DEC_BATCH = 128
DEC_SEQ = 8
PAST_LEN = 2048
PAGE_SIZE = 128

DIFF_HEADS = 4
DIFF_QK = 32
DIFF_V = 64
DIFF_WIDTH = DIFF_HEADS * DIFF_V
NSA_HEADS = 6
NSA_KV_HEADS = 2
NSA_GROUP = NSA_HEADS // NSA_KV_HEADS
NSA_HD = 64
NSA_WIDTH = NSA_HEADS * NSA_HD
NSA_KV_WIDTH = 2 * NSA_KV_HEADS * NSA_HD
CMP_LEN = 32
CMP_STRIDE = 16
SLC_LEN = 64
N_SEL = 16
WINDOW = 512
SSM_GROUPS = 24
SSM_CH = 16
SSM_STATE = 64
SSM_WIDTH = SSM_GROUPS * SSM_CH
MIX_WIDTH = DIFF_WIDTH + NSA_WIDTH + SSM_WIDTH
N_IN = 2 * DIFF_HEADS * 2 * DIFF_QK + DIFF_WIDTH + NSA_WIDTH + 3 * NSA_KV_WIDTH + 3 * NSA_HEADS + SSM_WIDTH
D_FF = 2816
Q_BLOCK = 128
EPS = 1e-5

kernel_name = 'hymba_diffattn_nsa_s5_macaron_step'


def rmsnorm(x, g):
    xf = x.astype(jnp.float32)
    y = xf * lax.rsqrt(jnp.mean(xf * xf, axis=-1, keepdims=True) + EPS)
    return (y * g.astype(jnp.float32)).astype(x.dtype)


def masked_softmax(s, mask):
    s = jnp.where(mask, s.astype(jnp.float32), -jnp.inf)
    m = jnp.max(s, axis=-1, keepdims=True)
    m = jnp.where(jnp.isfinite(m), m, 0.0)
    e = jnp.where(mask, jnp.exp(s - m), 0.0)
    return e / jnp.maximum(jnp.sum(e, axis=-1, keepdims=True), 1e-30)


def ffn_half(x, g, w_gate, w_up, w_down):
    h = rmsnorm(x, g)
    return x + 0.5 * ((jax.nn.silu(h @ w_gate) * (h @ w_up)) @ w_down)


def mixer_projection(x, norm_g, w_in):
    b, t, _ = x.shape
    sizes = (DIFF_HEADS * 2 * DIFF_QK, DIFF_HEADS * 2 * DIFF_QK, DIFF_WIDTH, NSA_WIDTH,
             NSA_KV_WIDTH, NSA_KV_WIDTH, NSA_KV_WIDTH, 3 * NSA_HEADS, SSM_WIDTH)
    offsets = np.cumsum(sizes)[:-1].tolist()
    dq, dk, dv, nq, ckv, skv, wkv, gate, u = jnp.split(rmsnorm(x, norm_g) @ w_in, offsets, axis=-1)
    kv5 = lambda z: z.reshape(b, t, 2, NSA_KV_HEADS, NSA_HD)
    dq = dq.reshape(b, t, DIFF_HEADS, 2, DIFF_QK)
    dkv = jnp.stack([dk.reshape(b, t, DIFF_HEADS, 2 * DIFF_QK), dv.reshape(b, t, DIFF_HEADS, DIFF_V)], axis=2)
    return dq, dkv, nq.reshape(b, t, NSA_HEADS, NSA_HD), kv5(ckv), kv5(skv), kv5(wkv), gate, u


def gather_pages(cache, page_table):
    rows = cache[page_table]
    return rows.reshape((page_table.shape[0], -1) + cache.shape[2:])


def diff_core(q, k, v, qpos, kpos, lam):
    s = jnp.einsum('bqhcd,bkhcd->bhcqk', q, k) * (DIFF_QK ** -0.5)
    p = masked_softmax(s, kpos[None, :] <= qpos[:, None])
    a = p[:, :, 0] - lam * p[:, :, 1]
    return jnp.einsum('bhqk,bkhd->bqhd', a.astype(v.dtype), v)


def diff_prompt(dq, dkv, lam):
    b, t = dq.shape[:2]
    k = dkv[:, :, 0].reshape(b, t, DIFF_HEADS, 2, DIFF_QK)
    v = dkv[:, :, 1]
    kpos = jnp.arange(t)

    def block(i):
        s0 = i * Q_BLOCK
        qb = lax.dynamic_slice_in_dim(dq, s0, Q_BLOCK, axis=1)
        return diff_core(qb, k, v, s0 + jnp.arange(Q_BLOCK), kpos, lam)

    o = lax.map(block, jnp.arange(t // Q_BLOCK))
    return jnp.moveaxis(o, 0, 1).reshape(b, t, DIFF_HEADS, DIFF_V)


def diff_sample(dq, dkv, past_dkv, lam):
    b, s = dq.shape[:2]
    past_len = past_dkv.shape[1]
    kv = jnp.concatenate([past_dkv, dkv], axis=1)
    t = kv.shape[1]
    k = kv[:, :, 0].reshape(b, t, DIFF_HEADS, 2, DIFF_QK)
    return diff_core(dq, k, kv[:, :, 1], past_len + jnp.arange(s), jnp.arange(t), lam)


def diff_finish(o, gain, lam_init):
    b, t = o.shape[:2]
    return (rmsnorm(o, gain.reshape(DIFF_HEADS, DIFF_V)) * (1.0 - lam_init)).reshape(b, t, DIFF_WIDTH)


def nsa_compress(kv, pe, w1, w2):
    b, t = kv.shape[:2]
    r = CMP_LEN // CMP_STRIDE
    ns = t // CMP_STRIDE
    nc = ns - r + 1
    ch = kv[:, :ns * CMP_STRIDE].reshape(b, ns, CMP_STRIDE, 2, NSA_KV_HEADS, NSA_HD)
    blk = jnp.concatenate([ch[:, i:i + nc] for i in range(r)], axis=2)
    blk = blk + jnp.transpose(pe, (1, 0, 2))[:, :, None, :]
    flat = jnp.transpose(blk, (0, 1, 3, 4, 2, 5)).reshape(b, nc, 2, NSA_KV_HEADS, CMP_LEN * NSA_HD)
    hid = jax.nn.gelu(jnp.einsum('bnckf,cfe->bncke', flat, w1))
    out = jnp.einsum('bncke,ced->bnckd', hid, w2)
    cend = jnp.arange(nc) * CMP_STRIDE + CMP_LEN - 1
    return out[:, :, 0], out[:, :, 1], cend


def slc_blocks(kv):
    b, t = kv.shape[:2]
    nb = -(-t // SLC_LEN)
    kv = jnp.pad(kv, ((0, 0), (0, nb * SLC_LEN - t), (0, 0), (0, 0), (0, 0)))
    blk = jnp.transpose(kv.reshape(b, nb, SLC_LEN, 2, NSA_KV_HEADS, NSA_HD), (3, 0, 4, 1, 2, 5))
    return blk[0], blk[1]


def overlap_matrix(nc, nb):
    cs = jnp.arange(nc) * CMP_STRIDE
    js = jnp.arange(nb) * SLC_LEN
    return ((cs[:, None] < js[None, :] + SLC_LEN) & (cs[:, None] + CMP_LEN > js[None, :])).astype(jnp.float32)


def nsa_core(q, qpos, kc, vc, cend, ks, vs, kw, vw, kwpos, gate):
    b, nq = q.shape[:2]
    qg = q.reshape(b, nq, NSA_KV_HEADS, NSA_GROUP, NSA_HD)
    scale = NSA_HD ** -0.5
    s = jnp.einsum('bqkgd,bnkd->bqkgn', qg, kc) * scale
    p_c = masked_softmax(s, (cend[None, :] <= qpos[:, None])[None, :, None, None, :])
    o_c = jnp.einsum('bqkgn,bnkd->bqkgd', p_c.astype(vc.dtype), vc)
    nb = ks.shape[2]
    imp = jnp.einsum('bqkn,nj->bqkj', jnp.sum(p_c, axis=3), overlap_matrix(kc.shape[1], nb))
    j = jnp.arange(nb)[None, :]
    jt = (qpos // SLC_LEN)[:, None]
    forced = ((j == 0) | (j == jt) | (j == jt - 1))[None, :, None, :]
    score = jnp.where(forced, jnp.inf, jnp.where((j <= jt)[None, :, None, :], imp, -jnp.inf))
    n_sel = min(N_SEL, nb)
    _, idx = lax.top_k(score, n_sel)
    bi = jnp.arange(b)[:, None, None, None]
    hi = jnp.arange(NSA_KV_HEADS)[None, None, :, None]
    ksel = ks[bi, hi, idx]
    vsel = vs[bi, hi, idx]
    s = jnp.einsum('bqkgd,bqknld->bqkgnl', qg, ksel) * scale
    kpos = idx[..., None] * SLC_LEN + jnp.arange(SLC_LEN)
    smask = (kpos <= qpos[None, :, None, None, None]).reshape(b, nq, NSA_KV_HEADS, 1, n_sel * SLC_LEN)
    p_s = masked_softmax(s.reshape(b, nq, NSA_KV_HEADS, NSA_GROUP, n_sel * SLC_LEN), smask).reshape(s.shape)
    o_s = jnp.einsum('bqkgnl,bqknld->bqkgd', p_s.astype(vsel.dtype), vsel)
    s = jnp.einsum('bqkgd,bwkd->bqkgw', qg, kw) * scale
    wmask = (kwpos[None, :] <= qpos[:, None]) & (kwpos[None, :] > qpos[:, None] - WINDOW) & (kwpos[None, :] >= 0)
    p_w = masked_softmax(s, wmask[None, :, None, None, :])
    o_w = jnp.einsum('bqkgw,bwkd->bqkgd', p_w.astype(vw.dtype), vw)
    g = jax.nn.sigmoid(gate.astype(jnp.float32)).reshape(b, nq, NSA_KV_HEADS, NSA_GROUP, 3)
    o = g[..., 0:1] * o_c + g[..., 1:2] * o_s + g[..., 2:3] * o_w
    return o.reshape(b, nq, NSA_WIDTH).astype(q.dtype)


def nsa_prompt(nq, ckv, skv, wkv, gate, pe, w1, w2):
    b, t = nq.shape[:2]
    kc, vc, cend = nsa_compress(ckv, pe, w1, w2)
    ks, vs = slc_blocks(skv)
    wpad = jnp.pad(wkv, ((0, 0), (WINDOW, 0), (0, 0), (0, 0), (0, 0)))

    def block(i):
        s0 = i * Q_BLOCK
        qb = lax.dynamic_slice_in_dim(nq, s0, Q_BLOCK, axis=1)
        gb = lax.dynamic_slice_in_dim(gate, s0, Q_BLOCK, axis=1)
        wb = lax.dynamic_slice_in_dim(wpad, s0, WINDOW + Q_BLOCK, axis=1)
        qpos = s0 + jnp.arange(Q_BLOCK)
        kwpos = s0 - WINDOW + jnp.arange(WINDOW + Q_BLOCK)
        return nsa_core(qb, qpos, kc, vc, cend, ks, vs, wb[:, :, 0], wb[:, :, 1], kwpos, gb)

    o = lax.map(block, jnp.arange(t // Q_BLOCK))
    return jnp.moveaxis(o, 0, 1).reshape(b, t, NSA_WIDTH), wkv[:, -min(WINDOW, t):]


def nsa_sample(nq, ckv, skv, wkv, gate, past_ckv, past_skv, win_buf, past_len, pe, w1, w2):
    s = nq.shape[1]
    wb = win_buf.shape[1]
    kc, vc, cend = nsa_compress(jnp.concatenate([past_ckv, ckv], axis=1), pe, w1, w2)
    ks, vs = slc_blocks(jnp.concatenate([past_skv, skv], axis=1))
    wfull = jnp.concatenate([win_buf, wkv], axis=1)
    qpos = past_len + jnp.arange(s)
    kwpos = past_len - wb + jnp.arange(wb + s)
    o = nsa_core(nq, qpos, kc, vc, cend, ks, vs, wfull[:, :, 0], wfull[:, :, 1], kwpos, gate)
    return o, wfull[:, -min(WINDOW, wb + s):]


def ssm_discretize(lam_re, lam_im, log_dt, b_re, b_im):
    f32 = jnp.float32
    lr = jnp.minimum(lam_re.astype(f32), -1e-4)
    li = lam_im.astype(f32)
    dt = jnp.exp(log_dt.astype(f32))[:, None]
    mag = jnp.exp(lr * dt)
    a_re, a_im = mag * jnp.cos(li * dt), mag * jnp.sin(li * dt)
    den = lr * lr + li * li
    nr = a_re - 1.0
    f_re = (nr * lr + a_im * li) / den
    f_im = (a_im * lr - nr * li) / den
    b_re, b_im = b_re.astype(f32), b_im.astype(f32)
    bb_re = f_re[..., None] * b_re - f_im[..., None] * b_im
    bb_im = f_re[..., None] * b_im + f_im[..., None] * b_re
    return a_re, a_im, bb_re, bb_im


def ssm_combine(e1, e2):
    a1r, a1i, b1r, b1i = e1
    a2r, a2i, b2r, b2i = e2
    return (a2r * a1r - a2i * a1i, a2r * a1i + a2i * a1r,
            a2r * b1r - a2i * b1i + b2r, a2r * b1i + a2i * b1r + b2i)


def ssm_mixer(u, h0_re, h0_im, disc, c_re, c_im, d, glu_w):
    f32 = jnp.float32
    a_re, a_im, bb_re, bb_im = disc
    b, t, _ = u.shape
    uf = u.astype(f32)
    ug = uf.reshape(b, t, SSM_GROUPS, SSM_CH)
    bu_re = jnp.einsum('gpc,btgc->btgp', bb_re, ug)
    bu_im = jnp.einsum('gpc,btgc->btgp', bb_im, ug)
    h0_re, h0_im = h0_re.astype(f32), h0_im.astype(f32)
    bu_re = bu_re.at[:, 0].add(a_re * h0_re - a_im * h0_im)
    bu_im = bu_im.at[:, 0].add(a_re * h0_im + a_im * h0_re)
    ar = jnp.broadcast_to(a_re, bu_re.shape)
    ai = jnp.broadcast_to(a_im, bu_im.shape)
    _, _, h_re, h_im = lax.associative_scan(ssm_combine, (ar, ai, bu_re, bu_im), axis=1)
    y = jnp.einsum('gcp,btgp->btgc', c_re.astype(f32), h_re) - jnp.einsum('gcp,btgp->btgc', c_im.astype(f32), h_im)
    y = y.reshape(b, t, SSM_WIDTH) + d.astype(f32) * uf
    z = y @ glu_w.astype(f32)
    out = z[..., :SSM_WIDTH] * jax.nn.sigmoid(z[..., SSM_WIDTH:])
    return out.astype(u.dtype), jnp.stack([h_re[:, -1], h_im[:, -1]], axis=1).astype(u.dtype)


def setup_inputs(seed: int = 0) -> dict:
    key = jax.random.key(seed)
    keys = iter(jax.random.split(key, 64))
    f32 = jnp.float32

    def nrm(shape, scale):
        return jax.random.normal(next(keys), shape, f32) * scale

    def gain(shape):
        return 1.0 + nrm(shape, 0.01)

    n_pages = PAST_LEN // PAGE_SIZE
    n_used = DEC_BATCH * n_pages
    n_pool = (5 * n_used + 3) // 4
    win_buf = min(WINDOW, PAST_LEN)
    G, P, C = SSM_GROUPS, SSM_STATE, SSM_CH
    page_table = jax.random.permutation(next(keys), n_pool)[:n_used].reshape(DEC_BATCH, n_pages).astype(jnp.int32)
    return {
        'x_prompt': nrm((BATCH, SEQ, D_MODEL), 1.0),
        'x_sample': nrm((DEC_BATCH, DEC_SEQ, D_MODEL), 1.0),
        'cache_diff_kv': nrm((DEPTH, n_pool, PAGE_SIZE, 2, DIFF_HEADS, DIFF_V), 1.0),
        'cache_nsa_cmp_kv': nrm((DEPTH, n_pool, PAGE_SIZE, 2, NSA_KV_HEADS, NSA_HD), 1.0),
        'cache_nsa_slc_kv': nrm((DEPTH, n_pool, PAGE_SIZE, 2, NSA_KV_HEADS, NSA_HD), 1.0),
        'state_nsa_win_kv': nrm((DEPTH, DEC_BATCH, win_buf, 2, NSA_KV_HEADS, NSA_HD), 1.0),
        'state_ssm': nrm((DEPTH, DEC_BATCH, 2, G, P), 0.3),
        'page_table': page_table,
        'ffn1_norm': gain((DEPTH, D_MODEL)),
        'ffn1_w_gate': nrm((DEPTH, D_MODEL, D_FF), D_MODEL ** -0.5),
        'ffn1_w_up': nrm((DEPTH, D_MODEL, D_FF), D_MODEL ** -0.5),
        'ffn1_w_down': nrm((DEPTH, D_FF, D_MODEL), D_FF ** -0.5),
        'mix_norm': gain((DEPTH, D_MODEL)),
        'w_in': nrm((DEPTH, D_MODEL, N_IN), D_MODEL ** -0.5),
        'diff_lambda': nrm((DEPTH, 4, DIFF_QK), 0.1),
        'diff_head_norm': gain((DEPTH, DIFF_WIDTH)),
        'nsa_cmp_pe': nrm((DEPTH, 2, CMP_LEN, NSA_HD), 0.02),
        'nsa_cmp_w1': nrm((DEPTH, 2, CMP_LEN * NSA_HD, NSA_HD), (CMP_LEN * NSA_HD) ** -0.5),
        'nsa_cmp_w2': nrm((DEPTH, 2, NSA_HD, NSA_HD), NSA_HD ** -0.5),
        'ssm_lambda_re': -0.5 + nrm((DEPTH, G, P), 0.01),
        'ssm_lambda_im': math.pi * jnp.arange(P, dtype=f32) + nrm((DEPTH, G, P), 0.01),
        'ssm_log_dt': jax.random.uniform(next(keys), (DEPTH, G), f32, math.log(1e-3), math.log(1e-1)),
        'ssm_b_re': nrm((DEPTH, G, P, C), (2 * C) ** -0.5),
        'ssm_b_im': nrm((DEPTH, G, P, C), (2 * C) ** -0.5),
        'ssm_c_re': nrm((DEPTH, G, C, P), P ** -0.5),
        'ssm_c_im': nrm((DEPTH, G, C, P), P ** -0.5),
        'ssm_d': nrm((DEPTH, SSM_WIDTH), 1.0),
        'ssm_glu_w': nrm((DEPTH, SSM_WIDTH, 2 * SSM_WIDTH), SSM_WIDTH ** -0.5),
        'w_out': nrm((DEPTH, MIX_WIDTH, D_MODEL), MIX_WIDTH ** -0.5),
        'ffn2_norm': gain((DEPTH, D_MODEL)),
        'ffn2_w_gate': nrm((DEPTH, D_MODEL, D_FF), D_MODEL ** -0.5),
        'ffn2_w_up': nrm((DEPTH, D_MODEL, D_FF), D_MODEL ** -0.5),
        'ffn2_w_down': nrm((DEPTH, D_FF, D_MODEL), D_FF ** -0.5),
        'final_norm': gain((D_MODEL,)),
    }


def reference(x_prompt, x_sample, cache_diff_kv, cache_nsa_cmp_kv, cache_nsa_slc_kv, state_nsa_win_kv,
              state_ssm, page_table, ffn1_norm, ffn1_w_gate, ffn1_w_up, ffn1_w_down, mix_norm, w_in,
              diff_lambda, diff_head_norm, nsa_cmp_pe, nsa_cmp_w1, nsa_cmp_w2, ssm_lambda_re, ssm_lambda_im,
              ssm_log_dt, ssm_b_re, ssm_b_im, ssm_c_re, ssm_c_im, ssm_d, ssm_glu_w, w_out,
              ffn2_norm, ffn2_w_gate, ffn2_w_up, ffn2_w_down, final_norm):
    f32 = jnp.float32
    n_pages = page_table.shape[1]
    past_len = n_pages * cache_diff_kv.shape[2]
    b_p = x_prompt.shape[0]
    xp, xs = x_prompt, x_sample
    dkv_p, dkv_s, ckv_p, ckv_s, skv_p, skv_s, win_p, win_s, ssm_p, ssm_s = ([] for _ in range(10))
    for l in range(DEPTH):
        lam_init = 0.8 - 0.6 * math.exp(-0.3 * l)
        lv = diff_lambda[l].astype(f32)
        lam = jnp.exp(jnp.sum(lv[0] * lv[1])) - jnp.exp(jnp.sum(lv[2] * lv[3])) + lam_init
        disc = ssm_discretize(ssm_lambda_re[l], ssm_lambda_im[l], ssm_log_dt[l], ssm_b_re[l], ssm_b_im[l])
        ssm_w = (ssm_c_re[l], ssm_c_im[l], ssm_d[l], ssm_glu_w[l])
        cmp_w = (nsa_cmp_pe[l], nsa_cmp_w1[l], nsa_cmp_w2[l])
        ffn1 = (ffn1_norm[l], ffn1_w_gate[l], ffn1_w_up[l], ffn1_w_down[l])
        ffn2 = (ffn2_norm[l], ffn2_w_gate[l], ffn2_w_up[l], ffn2_w_down[l])

        xp = ffn_half(xp, *ffn1)
        dq, dkv, nq, ckv, skv, wkv, gate, u = mixer_projection(xp, mix_norm[l], w_in[l])
        o_d = diff_prompt(dq, dkv, lam)
        o_n, win = nsa_prompt(nq, ckv, skv, wkv, gate, *cmp_w)
        h0 = jnp.zeros((b_p, SSM_GROUPS, SSM_STATE), f32)
        o_s, h = ssm_mixer(u, h0, h0, disc, *ssm_w)
        xp = xp + jnp.concatenate([diff_finish(o_d, diff_head_norm[l], lam_init), o_n, o_s], axis=-1) @ w_out[l]
        xp = ffn_half(xp, *ffn2)
        dkv_p.append(dkv); ckv_p.append(ckv); skv_p.append(skv); win_p.append(win); ssm_p.append(h)

        xs = ffn_half(xs, *ffn1)
        dq, dkv, nq, ckv, skv, wkv, gate, u = mixer_projection(xs, mix_norm[l], w_in[l])
        o_d = diff_sample(dq, dkv, gather_pages(cache_diff_kv[l], page_table), lam)
        o_n, win = nsa_sample(nq, ckv, skv, wkv, gate, gather_pages(cache_nsa_cmp_kv[l], page_table),
                              gather_pages(cache_nsa_slc_kv[l], page_table), state_nsa_win_kv[l], past_len, *cmp_w)
        o_s, h = ssm_mixer(u, state_ssm[l][:, 0], state_ssm[l][:, 1], disc, *ssm_w)
        xs = xs + jnp.concatenate([diff_finish(o_d, diff_head_norm[l], lam_init), o_n, o_s], axis=-1) @ w_out[l]
        xs = ffn_half(xs, *ffn2)
        dkv_s.append(dkv); ckv_s.append(ckv); skv_s.append(skv); win_s.append(win); ssm_s.append(h)

    y_prompt = rmsnorm(xp, final_norm)
    y_sample = rmsnorm(xs, final_norm)
    return (y_prompt, y_sample, jnp.stack(dkv_p), jnp.stack(dkv_s), jnp.stack(ckv_p), jnp.stack(ckv_s),
            jnp.stack(skv_p), jnp.stack(skv_s), jnp.stack(win_p), jnp.stack(win_s), jnp.stack(ssm_p), jnp.stack(ssm_s))
```

```python
import functools
import math

import jax
import jax.numpy as jnp
import numpy as np
from jax import lax
from jax.experimental import pallas as pl
from jax.experimental.pallas import tpu as pltpu

D_MODEL = 1024
DEPTH = 4
DIFF_HEADS = 4
DIFF_QK = 32
DIFF_V = 64
DIFF_WIDTH = DIFF_HEADS * DIFF_V
NSA_HEADS = 6
NSA_KV_HEADS = 2
NSA_GROUP = NSA_HEADS // NSA_KV_HEADS
NSA_HD = 64
NSA_WIDTH = NSA_HEADS * NSA_HD
NSA_KV_WIDTH = 2 * NSA_KV_HEADS * NSA_HD
CMP_LEN = 32
CMP_STRIDE = 16
SLC_LEN = 64
N_SEL = 16
WINDOW = 512
SSM_GROUPS = 24
SSM_CH = 16
SSM_STATE = 64
SSM_WIDTH = SSM_GROUPS * SSM_CH
MIX_WIDTH = DIFF_WIDTH + NSA_WIDTH + SSM_WIDTH
D_FF = 2816
Q_BLOCK = 128
EPS = 1e-5

VMEM_LIMIT_BYTES = 56 * 1024 * 1024
MXU_WIDTH = 256

F32 = jnp.float32
BF16 = jnp.bfloat16


def _rms_rows(x, g):
    return x * lax.rsqrt(jnp.mean(x * x, axis=-1, keepdims=True) + EPS) * g


FFN_TOKENS = 512
FFN_CHUNK = MXU_WIDTH


def _ffn_kernel(x_ref, g_ref, wg_ref, wu_ref, wd_ref, o_ref, h_scr, acc_scr):
    x = x_ref[...]
    h_scr[...] = _rms_rows(x, g_ref[...]).astype(BF16)
    acc_scr[...] = jnp.zeros_like(acc_scr)

    def chunk(j, carry):
        h = h_scr[...]
        gate = jnp.dot(h, wg_ref[j], preferred_element_type=F32)
        up = jnp.dot(h, wu_ref[j], preferred_element_type=F32)
        act = (gate * jax.nn.sigmoid(gate) * up).astype(BF16)
        acc_scr[...] += jnp.dot(act, wd_ref[j], preferred_element_type=F32)
        return carry

    lax.fori_loop(0, wg_ref.shape[0], chunk, 0)
    o_ref[...] = x + 0.5 * acc_scr[...]


def ffn_half(x, g, w_gate, w_up, w_down):
    n_tok, d = x.shape
    d_ff = w_gate.shape[1]
    n_chunk = d_ff // FFN_CHUNK
    assert n_tok % FFN_TOKENS == 0 and d_ff % FFN_CHUNK == 0
    wg = w_gate.astype(BF16).reshape(d, n_chunk, FFN_CHUNK).transpose(1, 0, 2)
    wu = w_up.astype(BF16).reshape(d, n_chunk, FFN_CHUNK).transpose(1, 0, 2)
    wd = w_down.astype(BF16).reshape(n_chunk, FFN_CHUNK, d)
    const3 = lambda i: (0, 0, 0)
    return pl.pallas_call(
        _ffn_kernel,
        out_shape=jax.ShapeDtypeStruct((n_tok, d), F32),
        grid=(n_tok // FFN_TOKENS,),
        in_specs=[
            pl.BlockSpec((FFN_TOKENS, d), lambda i: (i, 0)),
            pl.BlockSpec((1, d), lambda i: (0, 0)),
            pl.BlockSpec((n_chunk, d, FFN_CHUNK), const3, pipeline_mode=pl.Buffered(1)),
            pl.BlockSpec((n_chunk, d, FFN_CHUNK), const3, pipeline_mode=pl.Buffered(1)),
            pl.BlockSpec((n_chunk, FFN_CHUNK, d), const3, pipeline_mode=pl.Buffered(1)),
        ],
        out_specs=pl.BlockSpec((FFN_TOKENS, d), lambda i: (i, 0)),
        scratch_shapes=[pltpu.VMEM((FFN_TOKENS, d), BF16), pltpu.VMEM((FFN_TOKENS, d), F32)],
        compiler_params=pltpu.CompilerParams(
            dimension_semantics=("arbitrary",), vmem_limit_bytes=VMEM_LIMIT_BYTES),
        name="ffn_half",
    )(x, g.reshape(1, d), wg, wu, wd)


LANES = 128
NEG_INIT = -1e30
LOG2E = math.log2(math.e)


def _softmax_slabs(s_scr, p_scr, m_scr, l_scr, a_scr, n_slabs, tq, tk, mask_fn):
    n_tiles = tk // LANES
    for r in range(n_slabs):
        rows = slice(r * tq, (r + 1) * tq)
        tiles = []
        for j in range(n_tiles):
            t = s_scr[rows, j * LANES:(j + 1) * LANES]
            if mask_fn is not None:
                t = jnp.where(mask_fn(r, j), t, -jnp.inf)
            tiles.append(t)
        m_old = m_scr[rows, :]
        row_max = jnp.max(functools.reduce(jnp.maximum, tiles), axis=1, keepdims=True)
        m_new = jnp.maximum(m_old, row_max)
        alpha = jnp.exp2(m_old - m_new)
        probs = [jnp.exp2(t - m_new) for t in tiles]
        l_scr[rows, :] = alpha * l_scr[rows, :] + functools.reduce(jnp.add, probs)
        m_scr[rows, :] = m_new
        a_scr[rows, :] = alpha
        for j in range(n_tiles):
            p_scr[rows, j * LANES:(j + 1) * LANES] = probs[j].astype(BF16)


def _accumulate(acc_scr, a_scr, p_scr, v):
    pv = jnp.dot(p_scr[...], v, preferred_element_type=F32)
    alpha = a_scr[...]
    for c in range(acc_scr.shape[1] // LANES):
        cols = slice(c * LANES, (c + 1) * LANES)
        acc_scr[:, cols] = alpha * acc_scr[:, cols] + pv[:, cols]


def _reset_softmax(m_scr, l_scr, acc_scr):
    m_scr[...] = jnp.full_like(m_scr, NEG_INIT)
    l_scr[...] = jnp.zeros_like(l_scr)
    acc_scr[...] = jnp.zeros_like(acc_scr)


DIFF_Q_TILE = 128
DIFF_K_TILE = 512
N_DIFF_MAPS = 2 * DIFF_HEADS


def _diff_finish(acc_scr, l_scr, lam, gain, out_scale, tq):
    lane_o = lax.broadcasted_iota(jnp.int32, (tq, DIFF_WIDTH), 1)
    o = jnp.zeros((tq, DIFF_WIDTH), F32)
    inv_n = jnp.zeros((tq, DIFF_WIDTH), F32)
    for h in range(DIFF_HEADS):
        parts = []
        for c in range(2):
            rows = slice((2 * h + c) * tq, (2 * h + c + 1) * tq)
            l = jnp.maximum(jnp.sum(l_scr[rows, :], axis=1, keepdims=True), 1e-30)
            parts.append(acc_scr[rows, :] / l)
        in_head = (lane_o >= h * DIFF_V) & (lane_o < (h + 1) * DIFF_V)
        a = jnp.where(in_head, parts[0] - lam * parts[1], 0.0)
        ms = jnp.sum(a * a, axis=1, keepdims=True) * (1.0 / DIFF_V)
        o = o + a
        inv_n = jnp.where(in_head, lax.rsqrt(ms + EPS), inv_n)
    return o * inv_n * gain * out_scale


def _diff_prompt_kernel(lam_ref, dq_ref, dk_ref, dv_ref, gain_ref, o_ref,
                        q_scr, s_scr, p_scr, m_scr, l_scr, a_scr, acc_scr, *, out_scale):
    tq, tk = DIFF_Q_TILE, DIFF_K_TILE
    s0 = pl.program_id(1) * tq
    q = dq_ref[0] * (DIFF_QK ** -0.5 * LOG2E)
    lane = lax.broadcasted_iota(jnp.int32, q.shape, 1)
    for r in range(N_DIFF_MAPS):
        keep = (lane >= r * DIFF_QK) & (lane < (r + 1) * DIFF_QK)
        q_scr[r * tq:(r + 1) * tq, :] = jnp.where(keep, q, 0.0).astype(BF16)
    _reset_softmax(m_scr, l_scr, acc_scr)

    def step(kt, masked):
        k0 = pl.multiple_of(kt * tk, tk)
        s_scr[...] = lax.dot_general(q_scr[...], dk_ref[0, pl.ds(k0, tk), :], (((1,), (1,)), ((), ())),
                                     preferred_element_type=F32)
        mask_fn = None
        if masked:
            qpos = s0 + lax.broadcasted_iota(jnp.int32, (tq, LANES), 0)
            kpos = k0 + lax.broadcasted_iota(jnp.int32, (tq, LANES), 1)
            mask_fn = lambda r, j: kpos + j * LANES <= qpos
        _softmax_slabs(s_scr, p_scr, m_scr, l_scr, a_scr, N_DIFF_MAPS, tq, tk, mask_fn)
        _accumulate(acc_scr, a_scr, p_scr, dv_ref[0, pl.ds(k0, tk), :])

    kt_last = (s0 + tq - 1) // tk

    def body(kt, carry):
        step(kt, False)
        return carry

    lax.fori_loop(0, kt_last, body, 0)
    step(kt_last, True)
    o_ref[0] = _diff_finish(acc_scr, l_scr, lam_ref[0], gain_ref[...], out_scale, tq)


def diff_prompt_attention(dq, dk, dv, lam, gain, lam_init):
    b, t, w = dq.shape
    tq, tk = DIFF_Q_TILE, DIFF_K_TILE
    assert t % tk == 0 and tk % tq == 0
    rows = N_DIFF_MAPS * tq
    return pl.pallas_call(
        functools.partial(_diff_prompt_kernel, out_scale=1.0 - lam_init),
        out_shape=jax.ShapeDtypeStruct((b, t, w), F32),
        grid=(b, t // tq),
        in_specs=[
            pl.BlockSpec(memory_space=pltpu.SMEM),
            pl.BlockSpec((1, tq, w), lambda bi, i: (bi, i, 0)),
            pl.BlockSpec((1, t, w), lambda bi, i: (bi, 0, 0)),
            pl.BlockSpec((1, t, w), lambda bi, i: (bi, 0, 0)),
            pl.BlockSpec((1, w), lambda bi, i: (0, 0)),
        ],
        out_specs=pl.BlockSpec((1, tq, w), lambda bi, i: (bi, i, 0)),
        scratch_shapes=[pltpu.VMEM((rows, w), BF16), pltpu.VMEM((rows, tk), F32), pltpu.VMEM((rows, tk), BF16),
                        pltpu.VMEM((rows, LANES), F32), pltpu.VMEM((rows, LANES), F32),
                        pltpu.VMEM((rows, LANES), F32), pltpu.VMEM((rows, w), F32)],
        compiler_params=pltpu.CompilerParams(
            dimension_semantics=("arbitrary", "arbitrary"), vmem_limit_bytes=VMEM_LIMIT_BYTES),
        name="diff_prompt_attention",
    )(lam.reshape(1), dq, dk.astype(BF16), dv.astype(BF16), gain.reshape(1, w))


CMP_CHUNK_WIDTH = CMP_STRIDE * NSA_KV_WIDTH


def _compress_kernel(x_ref, pe_a_ref, pe_b_ref, w1a_ref, w1b_ref, w2_ref, o_ref):
    x = x_ref[...]
    rows = x.shape[0]
    first = jnp.dot((x + pe_a_ref[...]).astype(BF16), w1a_ref[...], preferred_element_type=F32)
    second = jnp.dot((x + pe_b_ref[...]).astype(BF16), w1b_ref[...], preferred_element_type=F32)
    hid = jax.nn.gelu(first + pltpu.roll(second, rows - 1, 0))
    o_ref[...] = jnp.dot(hid.astype(BF16), w2_ref[...], preferred_element_type=F32).astype(o_ref.dtype)


def _compress_weights(pe, w1, w2):
    eye = jnp.eye(2, dtype=F32)
    w1r = w1.reshape(2, CMP_LEN, NSA_HD, NSA_HD)
    big1 = jnp.einsum('clde,cx,ky->lckdxye', w1r, eye, eye).reshape(CMP_LEN, NSA_KV_WIDTH, NSA_KV_WIDTH)
    big1 = big1.astype(BF16)
    w1a = big1[:CMP_STRIDE].reshape(CMP_CHUNK_WIDTH, NSA_KV_WIDTH)
    w1b = big1[CMP_STRIDE:].reshape(CMP_CHUNK_WIDTH, NSA_KV_WIDTH)
    big2 = jnp.einsum('ced,cx,ky->ckexyd', w2, eye, eye).reshape(NSA_KV_WIDTH, NSA_KV_WIDTH).astype(BF16)
    pe16 = jnp.broadcast_to(jnp.transpose(pe, (1, 0, 2))[:, :, None, :], (CMP_LEN, 2, NSA_KV_HEADS, NSA_HD))
    pe_a = pe16[:CMP_STRIDE].reshape(1, CMP_CHUNK_WIDTH)
    pe_b = pe16[CMP_STRIDE:].reshape(1, CMP_CHUNK_WIDTH)
    return pe_a, pe_b, w1a, w1b, big2


def nsa_compress_all(kv, cmp_weights, rows_per_step):
    b, t, w = kv.shape
    ns = t // CMP_STRIDE
    assert t % CMP_STRIDE == 0 and rows_per_step % ns == 0 and (b * ns) % rows_per_step == 0
    x = kv.reshape(b * ns, CMP_CHUNK_WIDTH)
    pe_a, pe_b, w1a, w1b, w2 = cmp_weights
    const = lambda i: (0, 0)
    out = pl.pallas_call(
        _compress_kernel,
        out_shape=jax.ShapeDtypeStruct((b * ns, w), BF16),
        grid=(b * ns // rows_per_step,),
        in_specs=[
            pl.BlockSpec((rows_per_step, CMP_CHUNK_WIDTH), lambda i: (i, 0)),
            pl.BlockSpec((1, CMP_CHUNK_WIDTH), const), pl.BlockSpec((1, CMP_CHUNK_WIDTH), const),
            pl.BlockSpec((CMP_CHUNK_WIDTH, w), const), pl.BlockSpec((CMP_CHUNK_WIDTH, w), const),
            pl.BlockSpec((w, w), const),
        ],
        out_specs=pl.BlockSpec((rows_per_step, w), lambda i: (i, 0)),
        compiler_params=pltpu.CompilerParams(
            dimension_semantics=("arbitrary",), vmem_limit_bytes=VMEM_LIMIT_BYTES),
        name="nsa_compress",
    )(x, pe_a, pe_b, w1a, w1b, w2)
    return out.reshape(b, ns, w)


NSA_Q_TILE = 128
NSA_K_TILE = 512
NSA_ROWS = NSA_GROUP * NSA_Q_TILE
KV_LANES = NSA_KV_HEADS * NSA_HD
SLC_SHIFT = SLC_LEN.bit_length() - 1
assert 1 << SLC_SHIFT == SLC_LEN


def _split3(x):
    hi = x.astype(BF16)
    r1 = x - hi.astype(F32)
    mid = r1.astype(BF16)
    lo = (r1 - mid.astype(F32)).astype(BF16)
    return hi, mid, lo


def _selection_mask(score_t, st_scr, n_rows):
    st_scr[...] = score_t
    jidx = lax.broadcasted_iota(jnp.int32, score_t.shape, 0)

    def body(jp, cnt):
        row = jnp.broadcast_to(st_scr[pl.ds(jp, 1), :], score_t.shape)
        tie = jnp.where(jp < jidx, 1.0, 0.0)
        return cnt + jnp.where(row > score_t, 1.0, jnp.where(row == score_t, tie, 0.0))

    cnt = lax.fori_loop(0, n_rows, body, jnp.zeros(score_t.shape, F32))
    return jnp.where(cnt < float(N_SEL), 1.0, 0.0)


def _nsa_prompt_kernel(q_ref, gate_ref, kc_ref, vc_ref, ks_ref, vs_ref, kw_ref, vw_ref, o_ref,
                       q_scr, s_scr, p_scr, m_scr, l_scr, a_scr, acc_scr, st_scr):
    tq, tk = NSA_Q_TILE, NSA_K_TILE
    n_cmp = kc_ref.shape[1]
    n_blk = ks_ref.shape[1] // SLC_LEN
    s0 = pl.program_id(1) * tq
    qpos = s0 + lax.broadcasted_iota(jnp.int32, (tq, LANES), 0)
    lane = lax.broadcasted_iota(jnp.int32, (tq, LANES), 1)
    gates = jax.nn.sigmoid(gate_ref[0])
    kt_last = (s0 + tq - 1) // tk

    def finish():
        l = jnp.maximum(jnp.sum(l_scr[...], axis=1, keepdims=True), 1e-30)
        return acc_scr[...] / l

    for kvh in range(NSA_KV_HEADS):
        half = slice(kvh * NSA_HD, (kvh + 1) * NSA_HD)
        for g in range(NSA_GROUP):
            h = kvh * NSA_GROUP + g
            qh = q_ref[0, :, h * NSA_HD:(h + 1) * NSA_HD] * (NSA_HD ** -0.5 * LOG2E)
            zero = jnp.zeros_like(qh)
            qpad = jnp.concatenate([qh, zero] if kvh == 0 else [zero, qh], axis=1)
            q_scr[g * tq:(g + 1) * tq, :] = qpad.astype(BF16)

        n_ct = n_cmp // LANES
        s_scr[:, :n_cmp] = lax.dot_general(q_scr[...], kc_ref[0], (((1,), (1,)), ((), ())),
                                           preferred_element_type=F32)
        psum = [jnp.zeros((tq, LANES), F32) for _ in range(n_ct)]
        for g in range(NSA_GROUP):
            rows = slice(g * tq, (g + 1) * tq)
            masks = [(lane + j * LANES) * CMP_STRIDE + (CMP_LEN - 1) <= qpos for j in range(n_ct)]
            tiles = [jnp.where(masks[j], s_scr[rows, j * LANES:(j + 1) * LANES], -jnp.inf) for j in range(n_ct)]
            mx = jnp.max(functools.reduce(jnp.maximum, tiles), axis=1, keepdims=True)
            mx = jnp.where(mx == -jnp.inf, 0.0, mx)
            es = [jnp.where(masks[j], jnp.exp2(tiles[j] - mx), 0.0) for j in range(n_ct)]
            denom = jnp.maximum(jnp.sum(functools.reduce(jnp.add, es), axis=1, keepdims=True), 1e-30)
            for j in range(n_ct):
                pj = es[j] / denom
                psum[j] = psum[j] + pj
                p_scr[rows, j * LANES:(j + 1) * LANES] = pj.astype(BF16)
        o_cmp = jnp.dot(p_scr[:, :n_cmp], vc_ref[0], preferred_element_type=F32)

        cblk = lax.broadcasted_iota(jnp.int32, (n_cmp, n_blk), 0) * CMP_STRIDE
        sblk = lax.broadcasted_iota(jnp.int32, (n_cmp, n_blk), 1) * SLC_LEN
        overlap = jnp.where((cblk < sblk + SLC_LEN) & (cblk + CMP_LEN > sblk), 1.0, 0.0).astype(BF16)
        ps = jnp.concatenate(psum, axis=1)
        imp = sum(jnp.dot(t, overlap, preferred_element_type=F32) for t in _split3(ps))
        blk = lax.broadcasted_iota(jnp.int32, (tq, n_blk), 1)
        jt = jnp.right_shift(s0 + lax.broadcasted_iota(jnp.int32, (tq, n_blk), 0), SLC_SHIFT)
        forced = (blk == 0) | (blk == jt) | (blk == jt - 1)
        score = jnp.where(forced, jnp.inf, jnp.where(blk <= jt, imp, -jnp.inf))
        n_valid = (s0 + tq - 1) // SLC_LEN + 1
        sel_t = _selection_mask(score.T, st_scr, n_valid)
        sel = sel_t.T.astype(BF16)

        _reset_softmax(m_scr, l_scr, acc_scr)

        def slc_step(kt, causal):
            k0 = pl.multiple_of(kt * tk, tk)
            s_scr[...] = lax.dot_general(q_scr[...], ks_ref[0, pl.ds(k0, tk), :], (((1,), (1,)), ((), ())),
                                         preferred_element_type=F32)
            erow = lax.broadcasted_iota(jnp.int32, (n_blk, tk), 0)
            ecol = jnp.right_shift(k0 + lax.broadcasted_iota(jnp.int32, (n_blk, tk), 1), SLC_SHIFT)
            expand = jnp.where(erow == ecol, 1.0, 0.0).astype(BF16)
            member = jnp.dot(sel, expand, preferred_element_type=F32)

            def mask_fn(r, j):
                msk = member[:, j * LANES:(j + 1) * LANES] > 0.5
                if causal:
                    msk = msk & (k0 + j * LANES + lane <= qpos)
                return msk

            _softmax_slabs(s_scr, p_scr, m_scr, l_scr, a_scr, NSA_GROUP, tq, tk, mask_fn)
            _accumulate(acc_scr, a_scr, p_scr, vs_ref[0, pl.ds(k0, tk), :])

        def slc_body(kt, carry):
            slc_step(kt, False)
            return carry

        lax.fori_loop(0, kt_last, slc_body, 0)
        slc_step(kt_last, True)
        o_slc = finish()

        _reset_softmax(m_scr, l_scr, acc_scr)

        def win_step(kt):
            k0 = pl.multiple_of(kt * tk, tk)
            s_scr[...] = lax.dot_general(q_scr[...], kw_ref[0, pl.ds(k0, tk), :], (((1,), (1,)), ((), ())),
                                         preferred_element_type=F32)

            def mask_fn(r, j):
                kpos = k0 + j * LANES + lane
                return (kpos <= qpos) & (kpos > qpos - WINDOW)

            _softmax_slabs(s_scr, p_scr, m_scr, l_scr, a_scr, NSA_GROUP, tq, tk, mask_fn)
            _accumulate(acc_scr, a_scr, p_scr, vw_ref[0, pl.ds(k0, tk), :])

        def win_body(kt, carry):
            win_step(kt)
            return carry

        lax.fori_loop(jnp.maximum(s0 - WINDOW + 1, 0) // tk, kt_last + 1, win_body, 0)
        o_win = finish()

        for g in range(NSA_GROUP):
            rows = slice(g * tq, (g + 1) * tq)
            c = (kvh * NSA_GROUP + g) * 3
            mix = (gates[:, c:c + 1] * o_cmp[rows, half] + gates[:, c + 1:c + 2] * o_slc[rows, half]
                   + gates[:, c + 2:c + 3] * o_win[rows, half])
            h = kvh * NSA_GROUP + g
            o_ref[0, :, h * NSA_HD:(h + 1) * NSA_HD] = mix


def nsa_prompt_attention(nq, gate, kvc, skv, wkv):
    b, t, w = nq.shape
    tq, tk = NSA_Q_TILE, NSA_K_TILE
    n_cmp = kvc.shape[1]
    assert t % tk == 0 and tk % tq == 0 and n_cmp == t // CMP_STRIDE and n_cmp % LANES == 0 and n_cmp <= tk
    k_spec = lambda rows: pl.BlockSpec((1, rows, KV_LANES), lambda bi, i: (bi, 0, 0))
    v_spec = lambda rows: pl.BlockSpec((1, rows, KV_LANES), lambda bi, i: (bi, 0, 1))
    return pl.pallas_call(
        _nsa_prompt_kernel,
        out_shape=jax.ShapeDtypeStruct((b, t, w), F32),
        grid=(b, t // tq),
        in_specs=[
            pl.BlockSpec((1, tq, w), lambda bi, i: (bi, i, 0)),
            pl.BlockSpec((1, tq, gate.shape[2]), lambda bi, i: (bi, i, 0)),
            k_spec(n_cmp), v_spec(n_cmp), k_spec(t), v_spec(t), k_spec(t), v_spec(t),
        ],
        out_specs=pl.BlockSpec((1, tq, w), lambda bi, i: (bi, i, 0)),
        scratch_shapes=[pltpu.VMEM((NSA_ROWS, KV_LANES), BF16), pltpu.VMEM((NSA_ROWS, tk), F32),
                        pltpu.VMEM((NSA_ROWS, tk), BF16), pltpu.VMEM((NSA_ROWS, LANES), F32),
                        pltpu.VMEM((NSA_ROWS, LANES), F32), pltpu.VMEM((NSA_ROWS, LANES), F32),
                        pltpu.VMEM((NSA_ROWS, KV_LANES), F32), pltpu.VMEM((t // SLC_LEN, tq), F32)],
        compiler_params=pltpu.CompilerParams(
            dimension_semantics=("arbitrary", "arbitrary"), vmem_limit_bytes=VMEM_LIMIT_BYTES),
        name="nsa_prompt_attention",
    )(nq, gate, kvc, kvc, skv, skv, wkv, wkv)


def _gather_pages_kernel(pt_ref, cache_ref, out_ref, sems, *, n_pages, page):
    n_seq = out_ref.shape[0]

    def page_copy(b, p, slot):
        return pltpu.make_async_copy(cache_ref.at[pt_ref[b, p]], out_ref.at[b, pl.ds(p * page, page)],
                                     sems.at[slot])

    def row(b, carry):
        slot = b & 1
        for p in range(n_pages):
            page_copy(b, p, slot).start()

        @pl.when(b > 0)
        def _():
            for p in range(n_pages):
                page_copy(b - 1, p, 1 - slot).wait()
        return carry

    lax.fori_loop(0, n_seq, row, 0)
    for p in range(n_pages):
        page_copy(n_seq - 1, p, (n_seq - 1) & 1).wait()


def gather_pages(cache, page_table):
    n_seq, n_pages = page_table.shape
    page = cache.shape[1]
    width = math.prod(cache.shape[2:])
    out = pl.pallas_call(
        functools.partial(_gather_pages_kernel, n_pages=n_pages, page=page),
        out_shape=jax.ShapeDtypeStruct((n_seq, n_pages * page, width), cache.dtype),
        in_specs=[pl.BlockSpec(memory_space=pltpu.SMEM), pl.BlockSpec(memory_space=pl.ANY)],
        out_specs=pl.BlockSpec(memory_space=pl.ANY),
        scratch_shapes=[pltpu.SemaphoreType.DMA((2,))],
        name="gather_pages",
    )(page_table, cache.reshape(cache.shape[0], page, width))
    return out.reshape((n_seq, n_pages * page) + cache.shape[2:])


def rmsnorm(x, g):
    xf = x.astype(jnp.float32)
    y = xf * lax.rsqrt(jnp.mean(xf * xf, axis=-1, keepdims=True) + EPS)
    return (y * g.astype(jnp.float32)).astype(x.dtype)


def masked_softmax(s, mask):
    s = jnp.where(mask, s.astype(jnp.float32), -jnp.inf)
    m = jnp.max(s, axis=-1, keepdims=True)
    m = jnp.where(jnp.isfinite(m), m, 0.0)
    e = jnp.where(mask, jnp.exp(s - m), 0.0)
    return e / jnp.maximum(jnp.sum(e, axis=-1, keepdims=True), 1e-30)


def mixer_projection(x, norm_g, w_in):
    b, t, _ = x.shape
    sizes = (DIFF_HEADS * 2 * DIFF_QK, DIFF_HEADS * 2 * DIFF_QK, DIFF_WIDTH, NSA_WIDTH,
             NSA_KV_WIDTH, NSA_KV_WIDTH, NSA_KV_WIDTH, 3 * NSA_HEADS, SSM_WIDTH)
    offsets = np.cumsum(sizes)[:-1].tolist()
    dq, dk, dv, nq, ckv, skv, wkv, gate, u = jnp.split(rmsnorm(x, norm_g) @ w_in, offsets, axis=-1)
    kv5 = lambda z: z.reshape(b, t, 2, NSA_KV_HEADS, NSA_HD)
    dq = dq.reshape(b, t, DIFF_HEADS, 2, DIFF_QK)
    dkv = jnp.stack([dk.reshape(b, t, DIFF_HEADS, 2 * DIFF_QK), dv.reshape(b, t, DIFF_HEADS, DIFF_V)], axis=2)
    return dq, dkv, nq.reshape(b, t, NSA_HEADS, NSA_HD), kv5(ckv), kv5(skv), kv5(wkv), gate, u


def diff_core(q, k, v, qpos, kpos, lam):
    s = jnp.einsum('bqhcd,bkhcd->bhcqk', q, k) * (DIFF_QK ** -0.5)
    p = masked_softmax(s, kpos[None, :] <= qpos[:, None])
    a = p[:, :, 0] - lam * p[:, :, 1]
    return jnp.einsum('bhqk,bkhd->bqhd', a.astype(v.dtype), v)


def diff_prompt(dq, dkv, lam):
    b, t = dq.shape[:2]
    k = dkv[:, :, 0].reshape(b, t, DIFF_HEADS, 2, DIFF_QK)
    v = dkv[:, :, 1]
    kpos = jnp.arange(t)

    def block(i):
        s0 = i * Q_BLOCK
        qb = lax.dynamic_slice_in_dim(dq, s0, Q_BLOCK, axis=1)
        return diff_core(qb, k, v, s0 + jnp.arange(Q_BLOCK), kpos, lam)

    o = lax.map(block, jnp.arange(t // Q_BLOCK))
    return jnp.moveaxis(o, 0, 1).reshape(b, t, DIFF_HEADS, DIFF_V)


def diff_sample(dq, dkv, past_dkv, lam):
    b, s = dq.shape[:2]
    past_len = past_dkv.shape[1]
    kv = jnp.concatenate([past_dkv, dkv], axis=1)
    t = kv.shape[1]
    k = kv[:, :, 0].reshape(b, t, DIFF_HEADS, 2, DIFF_QK)
    return diff_core(dq, k, kv[:, :, 1], past_len + jnp.arange(s), jnp.arange(t), lam)


def diff_finish(o, gain, lam_init):
    b, t = o.shape[:2]
    return (rmsnorm(o, gain.reshape(DIFF_HEADS, DIFF_V)) * (1.0 - lam_init)).reshape(b, t, DIFF_WIDTH)


def nsa_compress(kv, pe, w1, w2):
    b, t = kv.shape[:2]
    r = CMP_LEN // CMP_STRIDE
    ns = t // CMP_STRIDE
    nc = ns - r + 1
    ch = kv[:, :ns * CMP_STRIDE].reshape(b, ns, CMP_STRIDE, 2, NSA_KV_HEADS, NSA_HD)
    blk = jnp.concatenate([ch[:, i:i + nc] for i in range(r)], axis=2)
    blk = blk + jnp.transpose(pe, (1, 0, 2))[:, :, None, :]
    flat = jnp.transpose(blk, (0, 1, 3, 4, 2, 5)).reshape(b, nc, 2, NSA_KV_HEADS, CMP_LEN * NSA_HD)
    hid = jax.nn.gelu(jnp.einsum('bnckf,cfe->bncke', flat, w1))
    out = jnp.einsum('bncke,ced->bnckd', hid, w2)
    cend = jnp.arange(nc) * CMP_STRIDE + CMP_LEN - 1
    return out[:, :, 0], out[:, :, 1], cend


def slc_blocks(kv):
    b, t = kv.shape[:2]
    nb = -(-t // SLC_LEN)
    kv = jnp.pad(kv, ((0, 0), (0, nb * SLC_LEN - t), (0, 0), (0, 0), (0, 0)))
    blk = jnp.transpose(kv.reshape(b, nb, SLC_LEN, 2, NSA_KV_HEADS, NSA_HD), (3, 0, 4, 1, 2, 5))
    return blk[0], blk[1]


def overlap_matrix(nc, nb):
    cs = jnp.arange(nc) * CMP_STRIDE
    js = jnp.arange(nb) * SLC_LEN
    return ((cs[:, None] < js[None, :] + SLC_LEN) & (cs[:, None] + CMP_LEN > js[None, :])).astype(jnp.float32)


def nsa_core(q, qpos, kc, vc, cend, ks, vs, kw, vw, kwpos, gate):
    b, nq = q.shape[:2]
    qg = q.reshape(b, nq, NSA_KV_HEADS, NSA_GROUP, NSA_HD)
    scale = NSA_HD ** -0.5
    s = jnp.einsum('bqkgd,bnkd->bqkgn', qg, kc) * scale
    p_c = masked_softmax(s, (cend[None, :] <= qpos[:, None])[None, :, None, None, :])
    o_c = jnp.einsum('bqkgn,bnkd->bqkgd', p_c.astype(vc.dtype), vc)
    nb = ks.shape[2]
    imp = jnp.einsum('bqkn,nj->bqkj', jnp.sum(p_c, axis=3), overlap_matrix(kc.shape[1], nb))
    j = jnp.arange(nb)[None, :]
    jt = (qpos // SLC_LEN)[:, None]
    forced = ((j == 0) | (j == jt) | (j == jt - 1))[None, :, None, :]
    score = jnp.where(forced, jnp.inf, jnp.where((j <= jt)[None, :, None, :], imp, -jnp.inf))
    n_sel = min(N_SEL, nb)
    _, idx = lax.top_k(score, n_sel)
    bi = jnp.arange(b)[:, None, None, None]
    hi = jnp.arange(NSA_KV_HEADS)[None, None, :, None]
    ksel = ks[bi, hi, idx]
    vsel = vs[bi, hi, idx]
    s = jnp.einsum('bqkgd,bqknld->bqkgnl', qg, ksel) * scale
    kpos = idx[..., None] * SLC_LEN + jnp.arange(SLC_LEN)
    smask = (kpos <= qpos[None, :, None, None, None]).reshape(b, nq, NSA_KV_HEADS, 1, n_sel * SLC_LEN)
    p_s = masked_softmax(s.reshape(b, nq, NSA_KV_HEADS, NSA_GROUP, n_sel * SLC_LEN), smask).reshape(s.shape)
    o_s = jnp.einsum('bqkgnl,bqknld->bqkgd', p_s.astype(vsel.dtype), vsel)
    s = jnp.einsum('bqkgd,bwkd->bqkgw', qg, kw) * scale
    wmask = (kwpos[None, :] <= qpos[:, None]) & (kwpos[None, :] > qpos[:, None] - WINDOW) & (kwpos[None, :] >= 0)
    p_w = masked_softmax(s, wmask[None, :, None, None, :])
    o_w = jnp.einsum('bqkgw,bwkd->bqkgd', p_w.astype(vw.dtype), vw)
    g = jax.nn.sigmoid(gate.astype(jnp.float32)).reshape(b, nq, NSA_KV_HEADS, NSA_GROUP, 3)
    o = g[..., 0:1] * o_c + g[..., 1:2] * o_s + g[..., 2:3] * o_w
    return o.reshape(b, nq, NSA_WIDTH).astype(q.dtype)


def nsa_prompt(nq, ckv, skv, wkv, gate, pe, w1, w2):
    b, t = nq.shape[:2]
    kc, vc, cend = nsa_compress(ckv, pe, w1, w2)
    ks, vs = slc_blocks(skv)
    wpad = jnp.pad(wkv, ((0, 0), (WINDOW, 0), (0, 0), (0, 0), (0, 0)))

    def block(i):
        s0 = i * Q_BLOCK
        qb = lax.dynamic_slice_in_dim(nq, s0, Q_BLOCK, axis=1)
        gb = lax.dynamic_slice_in_dim(gate, s0, Q_BLOCK, axis=1)
        wb = lax.dynamic_slice_in_dim(wpad, s0, WINDOW + Q_BLOCK, axis=1)
        qpos = s0 + jnp.arange(Q_BLOCK)
        kwpos = s0 - WINDOW + jnp.arange(WINDOW + Q_BLOCK)
        return nsa_core(qb, qpos, kc, vc, cend, ks, vs, wb[:, :, 0], wb[:, :, 1], kwpos, gb)

    o = lax.map(block, jnp.arange(t // Q_BLOCK))
    return jnp.moveaxis(o, 0, 1).reshape(b, t, NSA_WIDTH), wkv[:, -min(WINDOW, t):]


def nsa_sample(nq, ckv, skv, wkv, gate, past_ckv, past_skv, win_buf, past_len, pe, w1, w2):
    s = nq.shape[1]
    wb = win_buf.shape[1]
    kc, vc, cend = nsa_compress(jnp.concatenate([past_ckv, ckv], axis=1), pe, w1, w2)
    ks, vs = slc_blocks(jnp.concatenate([past_skv, skv], axis=1))
    wfull = jnp.concatenate([win_buf, wkv], axis=1)
    qpos = past_len + jnp.arange(s)
    kwpos = past_len - wb + jnp.arange(wb + s)
    o = nsa_core(nq, qpos, kc, vc, cend, ks, vs, wfull[:, :, 0], wfull[:, :, 1], kwpos, gate)
    return o, wfull[:, -min(WINDOW, wb + s):]


def ssm_discretize(lam_re, lam_im, log_dt, b_re, b_im):
    f32 = jnp.float32
    lr = jnp.minimum(lam_re.astype(f32), -1e-4)
    li = lam_im.astype(f32)
    dt = jnp.exp(log_dt.astype(f32))[:, None]
    mag = jnp.exp(lr * dt)
    a_re, a_im = mag * jnp.cos(li * dt), mag * jnp.sin(li * dt)
    den = lr * lr + li * li
    nr = a_re - 1.0
    f_re = (nr * lr + a_im * li) / den
    f_im = (a_im * lr - nr * li) / den
    b_re, b_im = b_re.astype(f32), b_im.astype(f32)
    bb_re = f_re[..., None] * b_re - f_im[..., None] * b_im
    bb_im = f_re[..., None] * b_im + f_im[..., None] * b_re
    return a_re, a_im, bb_re, bb_im


def ssm_combine(e1, e2):
    a1r, a1i, b1r, b1i = e1
    a2r, a2i, b2r, b2i = e2
    return (a2r * a1r - a2i * a1i, a2r * a1i + a2i * a1r,
            a2r * b1r - a2i * b1i + b2r, a2r * b1i + a2i * b1r + b2i)


def ssm_mixer(u, h0_re, h0_im, disc, c_re, c_im, d, glu_w):
    f32 = jnp.float32
    a_re, a_im, bb_re, bb_im = disc
    b, t, _ = u.shape
    uf = u.astype(f32)
    ug = uf.reshape(b, t, SSM_GROUPS, SSM_CH)
    bu_re = jnp.einsum('gpc,btgc->btgp', bb_re, ug)
    bu_im = jnp.einsum('gpc,btgc->btgp', bb_im, ug)
    h0_re, h0_im = h0_re.astype(f32), h0_im.astype(f32)
    bu_re = bu_re.at[:, 0].add(a_re * h0_re - a_im * h0_im)
    bu_im = bu_im.at[:, 0].add(a_re * h0_im + a_im * h0_re)
    ar = jnp.broadcast_to(a_re, bu_re.shape)
    ai = jnp.broadcast_to(a_im, bu_im.shape)
    _, _, h_re, h_im = lax.associative_scan(ssm_combine, (ar, ai, bu_re, bu_im), axis=1)
    y = jnp.einsum('gcp,btgp->btgc', c_re.astype(f32), h_re) - jnp.einsum('gcp,btgp->btgc', c_im.astype(f32), h_im)
    y = y.reshape(b, t, SSM_WIDTH) + d.astype(f32) * uf
    z = y @ glu_w.astype(f32)
    out = z[..., :SSM_WIDTH] * jax.nn.sigmoid(z[..., SSM_WIDTH:])
    return out.astype(u.dtype), jnp.stack([h_re[:, -1], h_im[:, -1]], axis=1).astype(u.dtype)


def kernel(x_prompt, x_sample, cache_diff_kv, cache_nsa_cmp_kv, cache_nsa_slc_kv, state_nsa_win_kv, state_ssm, page_table, ffn1_norm, ffn1_w_gate, ffn1_w_up, ffn1_w_down, mix_norm, w_in, diff_lambda, diff_head_norm, nsa_cmp_pe, nsa_cmp_w1, nsa_cmp_w2, ssm_lambda_re, ssm_lambda_im, ssm_log_dt, ssm_b_re, ssm_b_im, ssm_c_re, ssm_c_im, ssm_d, ssm_glu_w, w_out, ffn2_norm, ffn2_w_gate, ffn2_w_up, ffn2_w_down, final_norm):
    f32 = jnp.float32
    n_pages = page_table.shape[1]
    past_len = n_pages * cache_diff_kv.shape[2]
    b_p, t_p, d = x_prompt.shape
    b_s, t_s, _ = x_sample.shape
    n_p = b_p * t_p
    x = jnp.concatenate([x_prompt.reshape(n_p, d), x_sample.reshape(b_s * t_s, d)], axis=0)
    dkv_p, dkv_s, ckv_p, ckv_s, skv_p, skv_s, win_p, win_s, ssm_p, ssm_s = ([] for _ in range(10))
    for l in range(DEPTH):
        lam_init = 0.8 - 0.6 * math.exp(-0.3 * l)
        lv = diff_lambda[l].astype(f32)
        lam = jnp.exp(jnp.sum(lv[0] * lv[1])) - jnp.exp(jnp.sum(lv[2] * lv[3])) + lam_init
        disc = ssm_discretize(ssm_lambda_re[l], ssm_lambda_im[l], ssm_log_dt[l], ssm_b_re[l], ssm_b_im[l])
        ssm_w = (ssm_c_re[l], ssm_c_im[l], ssm_d[l], ssm_glu_w[l])
        cmp_w = (nsa_cmp_pe[l], nsa_cmp_w1[l], nsa_cmp_w2[l])

        x = ffn_half(x, ffn1_norm[l], ffn1_w_gate[l], ffn1_w_up[l], ffn1_w_down[l])
        xp = x[:n_p].reshape(b_p, t_p, d)
        xs = x[n_p:].reshape(b_s, t_s, d)

        dq, dkv, nq, ckv, skv, wkv, gate, u = mixer_projection(xp, mix_norm[l], w_in[l])
        o_d = diff_prompt_attention(dq.reshape(b_p, t_p, DIFF_WIDTH), dkv[:, :, 0].reshape(b_p, t_p, DIFF_WIDTH),
                                    dkv[:, :, 1].reshape(b_p, t_p, DIFF_WIDTH), lam, diff_head_norm[l], lam_init)
        cmp_big = _compress_weights(*cmp_w)
        kvc = nsa_compress_all(ckv.reshape(b_p, t_p, NSA_KV_WIDTH), cmp_big, t_p // CMP_STRIDE)
        o_n = nsa_prompt_attention(nq.reshape(b_p, t_p, NSA_WIDTH), gate, kvc,
                                   skv.reshape(b_p, t_p, NSA_KV_WIDTH).astype(BF16),
                                   wkv.reshape(b_p, t_p, NSA_KV_WIDTH).astype(BF16))
        win = wkv[:, -min(WINDOW, t_p):]
        h0 = jnp.zeros((b_p, SSM_GROUPS, SSM_STATE), f32)
        o_s, h = ssm_mixer(u, h0, h0, disc, *ssm_w)
        xp = xp + jnp.concatenate([o_d, o_n, o_s], axis=-1) @ w_out[l]
        dkv_p.append(dkv); ckv_p.append(ckv); skv_p.append(skv); win_p.append(win); ssm_p.append(h)

        dq, dkv, nq, ckv, skv, wkv, gate, u = mixer_projection(xs, mix_norm[l], w_in[l])
        o_d = diff_sample(dq, dkv, gather_pages(cache_diff_kv[l], page_table), lam)
        o_n, win = nsa_sample(nq, ckv, skv, wkv, gate, gather_pages(cache_nsa_cmp_kv[l], page_table),
                              gather_pages(cache_nsa_slc_kv[l], page_table), state_nsa_win_kv[l], past_len, *cmp_w)
        o_s, h = ssm_mixer(u, state_ssm[l][:, 0], state_ssm[l][:, 1], disc, *ssm_w)
        xs = xs + jnp.concatenate([diff_finish(o_d, diff_head_norm[l], lam_init), o_n, o_s], axis=-1) @ w_out[l]
        dkv_s.append(dkv); ckv_s.append(ckv); skv_s.append(skv); win_s.append(win); ssm_s.append(h)

        x = jnp.concatenate([xp.reshape(n_p, d), xs.reshape(b_s * t_s, d)], axis=0)
        x = ffn_half(x, ffn2_norm[l], ffn2_w_gate[l], ffn2_w_up[l], ffn2_w_down[l])

    y = rmsnorm(x, final_norm)
    y_prompt = y[:n_p].reshape(b_p, t_p, d)
    y_sample = y[n_p:].reshape(b_s, t_s, d)
    return (y_prompt, y_sample, jnp.stack(dkv_p), jnp.stack(dkv_s), jnp.stack(ckv_p), jnp.stack(ckv_s),
            jnp.stack(skv_p), jnp.stack(skv_s), jnp.stack(win_p), jnp.stack(win_s), jnp.stack(ssm_p), jnp.stack(ssm_s))
```

```python
import functools
import math

import jax
import jax.numpy as jnp
import numpy as np
from jax import lax
from jax.experimental import pallas as pl
from jax.experimental.pallas import tpu as pltpu

D_MODEL = 1024
DEPTH = 4
DIFF_HEADS = 4
DIFF_QK = 32
DIFF_V = 64
DIFF_WIDTH = DIFF_HEADS * DIFF_V
NSA_HEADS = 6
NSA_KV_HEADS = 2
NSA_GROUP = NSA_HEADS // NSA_KV_HEADS
NSA_HD = 64
NSA_WIDTH = NSA_HEADS * NSA_HD
NSA_KV_WIDTH = 2 * NSA_KV_HEADS * NSA_HD
CMP_LEN = 32
CMP_STRIDE = 16
SLC_LEN = 64
N_SEL = 16
WINDOW = 512
SSM_GROUPS = 24
SSM_CH = 16
SSM_STATE = 64
SSM_WIDTH = SSM_GROUPS * SSM_CH
MIX_WIDTH = DIFF_WIDTH + NSA_WIDTH + SSM_WIDTH
D_FF = 2816
Q_BLOCK = 128
EPS = 1e-5

VMEM_LIMIT_BYTES = 56 * 1024 * 1024
MXU_WIDTH = 256

F32 = jnp.float32
BF16 = jnp.bfloat16


def _rms_rows(x, g):
    return x * lax.rsqrt(jnp.mean(x * x, axis=-1, keepdims=True) + EPS) * g


FFN_TOKENS = 512
FFN_CHUNK = MXU_WIDTH


def _ffn_kernel(x_ref, g_ref, wg_ref, wu_ref, wd_ref, o_ref, h_scr, acc_scr):
    x = x_ref[...]
    h_scr[...] = _rms_rows(x, g_ref[...]).astype(BF16)
    acc_scr[...] = jnp.zeros_like(acc_scr)

    def chunk(j, carry):
        h = h_scr[...]
        gate = jnp.dot(h, wg_ref[j], preferred_element_type=F32)
        up = jnp.dot(h, wu_ref[j], preferred_element_type=F32)
        act = (gate * jax.nn.sigmoid(gate) * up).astype(BF16)
        acc_scr[...] += jnp.dot(act, wd_ref[j], preferred_element_type=F32)
        return carry

    lax.fori_loop(0, wg_ref.shape[0], chunk, 0)
    o_ref[...] = x + 0.5 * acc_scr[...]


def ffn_half(x, g, w_gate, w_up, w_down):
    n_tok, d = x.shape
    d_ff = w_gate.shape[1]
    n_chunk = d_ff // FFN_CHUNK
    assert n_tok % FFN_TOKENS == 0 and d_ff % FFN_CHUNK == 0
    wg = w_gate.astype(BF16).reshape(d, n_chunk, FFN_CHUNK).transpose(1, 0, 2)
    wu = w_up.astype(BF16).reshape(d, n_chunk, FFN_CHUNK).transpose(1, 0, 2)
    wd = w_down.astype(BF16).reshape(n_chunk, FFN_CHUNK, d)
    const3 = lambda i: (0, 0, 0)
    return pl.pallas_call(
        _ffn_kernel,
        out_shape=jax.ShapeDtypeStruct((n_tok, d), F32),
        grid=(n_tok // FFN_TOKENS,),
        in_specs=[
            pl.BlockSpec((FFN_TOKENS, d), lambda i: (i, 0)),
            pl.BlockSpec((1, d), lambda i: (0, 0)),
            pl.BlockSpec((n_chunk, d, FFN_CHUNK), const3, pipeline_mode=pl.Buffered(1)),
            pl.BlockSpec((n_chunk, d, FFN_CHUNK), const3, pipeline_mode=pl.Buffered(1)),
            pl.BlockSpec((n_chunk, FFN_CHUNK, d), const3, pipeline_mode=pl.Buffered(1)),
        ],
        out_specs=pl.BlockSpec((FFN_TOKENS, d), lambda i: (i, 0)),
        scratch_shapes=[pltpu.VMEM((FFN_TOKENS, d), BF16), pltpu.VMEM((FFN_TOKENS, d), F32)],
        compiler_params=pltpu.CompilerParams(
            dimension_semantics=("arbitrary",), vmem_limit_bytes=VMEM_LIMIT_BYTES),
        name="ffn_half",
    )(x, g.reshape(1, d), wg, wu, wd)


LANES = 128
NEG_INIT = -1e30
LOG2E = math.log2(math.e)


def _softmax_slabs(s_scr, p_scr, m_scr, l_scr, a_scr, n_slabs, tq, tk, mask_fn):
    n_tiles = tk // LANES
    for r in range(n_slabs):
        rows = slice(r * tq, (r + 1) * tq)
        tiles = []
        for j in range(n_tiles):
            t = s_scr[rows, j * LANES:(j + 1) * LANES]
            if mask_fn is not None:
                t = jnp.where(mask_fn(r, j), t, -jnp.inf)
            tiles.append(t)
        m_old = m_scr[rows, :]
        row_max = jnp.max(functools.reduce(jnp.maximum, tiles), axis=1, keepdims=True)
        m_new = jnp.maximum(m_old, row_max)
        alpha = jnp.exp2(m_old - m_new)
        probs = [jnp.exp2(t - m_new) for t in tiles]
        l_scr[rows, :] = alpha * l_scr[rows, :] + functools.reduce(jnp.add, probs)
        m_scr[rows, :] = m_new
        a_scr[rows, :] = alpha
        for j in range(n_tiles):
            p_scr[rows, j * LANES:(j + 1) * LANES] = probs[j].astype(BF16)


def _accumulate(acc_scr, a_scr, p_scr, v):
    pv = jnp.dot(p_scr[...], v, preferred_element_type=F32)
    alpha = a_scr[...]
    for c in range(acc_scr.shape[1] // LANES):
        cols = slice(c * LANES, (c + 1) * LANES)
        acc_scr[:, cols] = alpha * acc_scr[:, cols] + pv[:, cols]


def _reset_softmax(m_scr, l_scr, acc_scr):
    m_scr[...] = jnp.full_like(m_scr, NEG_INIT)
    l_scr[...] = jnp.zeros_like(l_scr)
    acc_scr[...] = jnp.zeros_like(acc_scr)


DIFF_Q_TILE = 128
DIFF_K_TILE = 512
N_DIFF_MAPS = 2 * DIFF_HEADS


def _diff_finish(acc_scr, l_scr, lam, gain, out_scale, tq):
    lane_o = lax.broadcasted_iota(jnp.int32, (tq, DIFF_WIDTH), 1)
    o = jnp.zeros((tq, DIFF_WIDTH), F32)
    inv_n = jnp.zeros((tq, DIFF_WIDTH), F32)
    for h in range(DIFF_HEADS):
        parts = []
        for c in range(2):
            rows = slice((2 * h + c) * tq, (2 * h + c + 1) * tq)
            l = jnp.maximum(jnp.sum(l_scr[rows, :], axis=1, keepdims=True), 1e-30)
            parts.append(acc_scr[rows, :] / l)
        in_head = (lane_o >= h * DIFF_V) & (lane_o < (h + 1) * DIFF_V)
        a = jnp.where(in_head, parts[0] - lam * parts[1], 0.0)
        ms = jnp.sum(a * a, axis=1, keepdims=True) * (1.0 / DIFF_V)
        o = o + a
        inv_n = jnp.where(in_head, lax.rsqrt(ms + EPS), inv_n)
    return o * inv_n * gain * out_scale


def _diff_prompt_kernel(lam_ref, dq_ref, dk_ref, dv_ref, gain_ref, o_ref,
                        q_scr, s_scr, p_scr, m_scr, l_scr, a_scr, acc_scr, *, out_scale):
    tq, tk = DIFF_Q_TILE, DIFF_K_TILE
    s0 = pl.program_id(1) * tq
    q = dq_ref[0] * (DIFF_QK ** -0.5 * LOG2E)
    lane = lax.broadcasted_iota(jnp.int32, q.shape, 1)
    for r in range(N_DIFF_MAPS):
        keep = (lane >= r * DIFF_QK) & (lane < (r + 1) * DIFF_QK)
        q_scr[r * tq:(r + 1) * tq, :] = jnp.where(keep, q, 0.0).astype(BF16)
    _reset_softmax(m_scr, l_scr, acc_scr)

    def step(kt, masked):
        k0 = pl.multiple_of(kt * tk, tk)
        s_scr[...] = lax.dot_general(q_scr[...], dk_ref[0, pl.ds(k0, tk), :], (((1,), (1,)), ((), ())),
                                     preferred_element_type=F32)
        mask_fn = None
        if masked:
            qpos = s0 + lax.broadcasted_iota(jnp.int32, (tq, LANES), 0)
            kpos = k0 + lax.broadcasted_iota(jnp.int32, (tq, LANES), 1)
            mask_fn = lambda r, j: kpos + j * LANES <= qpos
        _softmax_slabs(s_scr, p_scr, m_scr, l_scr, a_scr, N_DIFF_MAPS, tq, tk, mask_fn)
        _accumulate(acc_scr, a_scr, p_scr, dv_ref[0, pl.ds(k0, tk), :])

    kt_last = (s0 + tq - 1) // tk

    def body(kt, carry):
        step(kt, False)
        return carry

    lax.fori_loop(0, kt_last, body, 0)
    step(kt_last, True)
    o_ref[0] = _diff_finish(acc_scr, l_scr, lam_ref[0], gain_ref[...], out_scale, tq)


def diff_prompt_attention(dq, dk, dv, lam, gain, lam_init):
    b, t, w = dq.shape
    tq, tk = DIFF_Q_TILE, DIFF_K_TILE
    assert t % tk == 0 and tk % tq == 0
    rows = N_DIFF_MAPS * tq
    return pl.pallas_call(
        functools.partial(_diff_prompt_kernel, out_scale=1.0 - lam_init),
        out_shape=jax.ShapeDtypeStruct((b, t, w), F32),
        grid=(b, t // tq),
        in_specs=[
            pl.BlockSpec(memory_space=pltpu.SMEM),
            pl.BlockSpec((1, tq, w), lambda bi, i: (bi, i, 0)),
            pl.BlockSpec((1, t, w), lambda bi, i: (bi, 0, 0)),
            pl.BlockSpec((1, t, w), lambda bi, i: (bi, 0, 0)),
            pl.BlockSpec((1, w), lambda bi, i: (0, 0)),
        ],
        out_specs=pl.BlockSpec((1, tq, w), lambda bi, i: (bi, i, 0)),
        scratch_shapes=[pltpu.VMEM((rows, w), BF16), pltpu.VMEM((rows, tk), F32), pltpu.VMEM((rows, tk), BF16),
                        pltpu.VMEM((rows, LANES), F32), pltpu.VMEM((rows, LANES), F32),
                        pltpu.VMEM((rows, LANES), F32), pltpu.VMEM((rows, w), F32)],
        compiler_params=pltpu.CompilerParams(
            dimension_semantics=("arbitrary", "arbitrary"), vmem_limit_bytes=VMEM_LIMIT_BYTES),
        name="diff_prompt_attention",
    )(lam.reshape(1), dq, dk.astype(BF16), dv.astype(BF16), gain.reshape(1, w))


CMP_CHUNK_WIDTH = CMP_STRIDE * NSA_KV_WIDTH


def _compress_kernel(x_ref, pe_a_ref, pe_b_ref, w1a_ref, w1b_ref, w2_ref, o_ref):
    x = x_ref[...]
    rows = x.shape[0]
    first = jnp.dot((x + pe_a_ref[...]).astype(BF16), w1a_ref[...], preferred_element_type=F32)
    second = jnp.dot((x + pe_b_ref[...]).astype(BF16), w1b_ref[...], preferred_element_type=F32)
    hid = jax.nn.gelu(first + pltpu.roll(second, rows - 1, 0))
    o_ref[...] = jnp.dot(hid.astype(BF16), w2_ref[...], preferred_element_type=F32).astype(o_ref.dtype)


def _compress_weights(pe, w1, w2):
    eye = jnp.eye(2, dtype=F32)
    w1r = w1.reshape(2, CMP_LEN, NSA_HD, NSA_HD)
    big1 = jnp.einsum('clde,cx,ky->lckdxye', w1r, eye, eye).reshape(CMP_LEN, NSA_KV_WIDTH, NSA_KV_WIDTH)
    big1 = big1.astype(BF16)
    w1a = big1[:CMP_STRIDE].reshape(CMP_CHUNK_WIDTH, NSA_KV_WIDTH)
    w1b = big1[CMP_STRIDE:].reshape(CMP_CHUNK_WIDTH, NSA_KV_WIDTH)
    big2 = jnp.einsum('ced,cx,ky->ckexyd', w2, eye, eye).reshape(NSA_KV_WIDTH, NSA_KV_WIDTH).astype(BF16)
    pe16 = jnp.broadcast_to(jnp.transpose(pe, (1, 0, 2))[:, :, None, :], (CMP_LEN, 2, NSA_KV_HEADS, NSA_HD))
    pe_a = pe16[:CMP_STRIDE].reshape(1, CMP_CHUNK_WIDTH)
    pe_b = pe16[CMP_STRIDE:].reshape(1, CMP_CHUNK_WIDTH)
    return pe_a, pe_b, w1a, w1b, big2


def nsa_compress_all(kv, cmp_weights, rows_per_step):
    b, t, w = kv.shape
    ns = t // CMP_STRIDE
    assert t % CMP_STRIDE == 0 and rows_per_step % ns == 0 and (b * ns) % rows_per_step == 0
    x = kv.reshape(b * ns, CMP_CHUNK_WIDTH)
    pe_a, pe_b, w1a, w1b, w2 = cmp_weights
    const = lambda i: (0, 0)
    out = pl.pallas_call(
        _compress_kernel,
        out_shape=jax.ShapeDtypeStruct((b * ns, w), BF16),
        grid=(b * ns // rows_per_step,),
        in_specs=[
            pl.BlockSpec((rows_per_step, CMP_CHUNK_WIDTH), lambda i: (i, 0)),
            pl.BlockSpec((1, CMP_CHUNK_WIDTH), const), pl.BlockSpec((1, CMP_CHUNK_WIDTH), const),
            pl.BlockSpec((CMP_CHUNK_WIDTH, w), const), pl.BlockSpec((CMP_CHUNK_WIDTH, w), const),
            pl.BlockSpec((w, w), const),
        ],
        out_specs=pl.BlockSpec((rows_per_step, w), lambda i: (i, 0)),
        compiler_params=pltpu.CompilerParams(
            dimension_semantics=("arbitrary",), vmem_limit_bytes=VMEM_LIMIT_BYTES),
        name="nsa_compress",
    )(x, pe_a, pe_b, w1a, w1b, w2)
    return out.reshape(b, ns, w)


NSA_Q_TILE = 128
NSA_K_TILE = 512
NSA_ROWS = NSA_GROUP * NSA_Q_TILE
KV_LANES = NSA_KV_HEADS * NSA_HD
SLC_SHIFT = SLC_LEN.bit_length() - 1
assert 1 << SLC_SHIFT == SLC_LEN


def _split3(x):
    hi = x.astype(BF16)
    r1 = x - hi.astype(F32)
    mid = r1.astype(BF16)
    lo = (r1 - mid.astype(F32)).astype(BF16)
    return hi, mid, lo


def _selection_mask(score_t, st_scr, n_rows):
    st_scr[...] = score_t
    jidx = lax.broadcasted_iota(jnp.int32, score_t.shape, 0)

    def body(jp, cnt):
        row = jnp.broadcast_to(st_scr[pl.ds(jp, 1), :], score_t.shape)
        tie = jnp.where(jp < jidx, 1.0, 0.0)
        return cnt + jnp.where(row > score_t, 1.0, jnp.where(row == score_t, tie, 0.0))

    cnt = lax.fori_loop(0, n_rows, body, jnp.zeros(score_t.shape, F32))
    return jnp.where(cnt < float(N_SEL), 1.0, 0.0)


def _nsa_prompt_kernel(q_ref, gate_ref, kc_ref, vc_ref, ks_ref, vs_ref, kw_ref, vw_ref, o_ref,
                       q_scr, s_scr, p_scr, m_scr, l_scr, a_scr, acc_scr, st_scr):
    tq, tk = NSA_Q_TILE, NSA_K_TILE
    n_cmp = kc_ref.shape[1]
    n_blk = ks_ref.shape[1] // SLC_LEN
    s0 = pl.program_id(1) * tq
    qpos = s0 + lax.broadcasted_iota(jnp.int32, (tq, LANES), 0)
    lane = lax.broadcasted_iota(jnp.int32, (tq, LANES), 1)
    gates = jax.nn.sigmoid(gate_ref[0])
    kt_last = (s0 + tq - 1) // tk

    def finish():
        l = jnp.maximum(jnp.sum(l_scr[...], axis=1, keepdims=True), 1e-30)
        return acc_scr[...] / l

    for kvh in range(NSA_KV_HEADS):
        half = slice(kvh * NSA_HD, (kvh + 1) * NSA_HD)
        for g in range(NSA_GROUP):
            h = kvh * NSA_GROUP + g
            qh = q_ref[0, :, h * NSA_HD:(h + 1) * NSA_HD] * (NSA_HD ** -0.5 * LOG2E)
            zero = jnp.zeros_like(qh)
            qpad = jnp.concatenate([qh, zero] if kvh == 0 else [zero, qh], axis=1)
            q_scr[g * tq:(g + 1) * tq, :] = qpad.astype(BF16)

        n_ct = n_cmp // LANES
        s_scr[:, :n_cmp] = lax.dot_general(q_scr[...], kc_ref[0], (((1,), (1,)), ((), ())),
                                           preferred_element_type=F32)
        psum = [jnp.zeros((tq, LANES), F32) for _ in range(n_ct)]
        for g in range(NSA_GROUP):
            rows = slice(g * tq, (g + 1) * tq)
            masks = [(lane + j * LANES) * CMP_STRIDE + (CMP_LEN - 1) <= qpos for j in range(n_ct)]
            tiles = [jnp.where(masks[j], s_scr[rows, j * LANES:(j + 1) * LANES], -jnp.inf) for j in range(n_ct)]
            mx = jnp.max(functools.reduce(jnp.maximum, tiles), axis=1, keepdims=True)
            mx = jnp.where(mx == -jnp.inf, 0.0, mx)
            es = [jnp.where(masks[j], jnp.exp2(tiles[j] - mx), 0.0) for j in range(n_ct)]
            denom = jnp.maximum(jnp.sum(functools.reduce(jnp.add, es), axis=1, keepdims=True), 1e-30)
            for j in range(n_ct):
                pj = es[j] / denom
                psum[j] = psum[j] + pj
                p_scr[rows, j * LANES:(j + 1) * LANES] = pj.astype(BF16)
        o_cmp = jnp.dot(p_scr[:, :n_cmp], vc_ref[0], preferred_element_type=F32)

        cblk = lax.broadcasted_iota(jnp.int32, (n_cmp, n_blk), 0) * CMP_STRIDE
        sblk = lax.broadcasted_iota(jnp.int32, (n_cmp, n_blk), 1) * SLC_LEN
        overlap = jnp.where((cblk < sblk + SLC_LEN) & (cblk + CMP_LEN > sblk), 1.0, 0.0).astype(BF16)
        ps = jnp.concatenate(psum, axis=1)
        imp = sum(jnp.dot(t, overlap, preferred_element_type=F32) for t in _split3(ps))
        blk = lax.broadcasted_iota(jnp.int32, (tq, n_blk), 1)
        jt = jnp.right_shift(s0 + lax.broadcasted_iota(jnp.int32, (tq, n_blk), 0), SLC_SHIFT)
        forced = (blk == 0) | (blk == jt) | (blk == jt - 1)
        score = jnp.where(forced, jnp.inf, jnp.where(blk <= jt, imp, -jnp.inf))
        n_valid = (s0 + tq - 1) // SLC_LEN + 1
        sel_t = _selection_mask(score.T, st_scr, n_valid)
        sel = sel_t.T.astype(BF16)

        _reset_softmax(m_scr, l_scr, acc_scr)

        def slc_step(kt, causal):
            k0 = pl.multiple_of(kt * tk, tk)
            s_scr[...] = lax.dot_general(q_scr[...], ks_ref[0, pl.ds(k0, tk), :], (((1,), (1,)), ((), ())),
                                         preferred_element_type=F32)
            erow = lax.broadcasted_iota(jnp.int32, (n_blk, tk), 0)
            ecol = jnp.right_shift(k0 + lax.broadcasted_iota(jnp.int32, (n_blk, tk), 1), SLC_SHIFT)
            expand = jnp.where(erow == ecol, 1.0, 0.0).astype(BF16)
            member = jnp.dot(sel, expand, preferred_element_type=F32)

            def mask_fn(r, j):
                msk = member[:, j * LANES:(j + 1) * LANES] > 0.5
                if causal:
                    msk = msk & (k0 + j * LANES + lane <= qpos)
                return msk

            _softmax_slabs(s_scr, p_scr, m_scr, l_scr, a_scr, NSA_GROUP, tq, tk, mask_fn)
            _accumulate(acc_scr, a_scr, p_scr, vs_ref[0, pl.ds(k0, tk), :])

        def slc_body(kt, carry):
            slc_step(kt, False)
            return carry

        lax.fori_loop(0, kt_last, slc_body, 0)
        slc_step(kt_last, True)
        o_slc = finish()

        _reset_softmax(m_scr, l_scr, acc_scr)

        def win_step(kt):
            k0 = pl.multiple_of(kt * tk, tk)
            s_scr[...] = lax.dot_general(q_scr[...], kw_ref[0, pl.ds(k0, tk), :], (((1,), (1,)), ((), ())),
                                         preferred_element_type=F32)

            def mask_fn(r, j):
                kpos = k0 + j * LANES + lane
                return (kpos <= qpos) & (kpos > qpos - WINDOW)

            _softmax_slabs(s_scr, p_scr, m_scr, l_scr, a_scr, NSA_GROUP, tq, tk, mask_fn)
            _accumulate(acc_scr, a_scr, p_scr, vw_ref[0, pl.ds(k0, tk), :])

        def win_body(kt, carry):
            win_step(kt)
            return carry

        lax.fori_loop(jnp.maximum(s0 - WINDOW + 1, 0) // tk, kt_last + 1, win_body, 0)
        o_win = finish()

        for g in range(NSA_GROUP):
            rows = slice(g * tq, (g + 1) * tq)
            c = (kvh * NSA_GROUP + g) * 3
            mix = (gates[:, c:c + 1] * o_cmp[rows, half] + gates[:, c + 1:c + 2] * o_slc[rows, half]
                   + gates[:, c + 2:c + 3] * o_win[rows, half])
            h = kvh * NSA_GROUP + g
            o_ref[0, :, h * NSA_HD:(h + 1) * NSA_HD] = mix


def nsa_prompt_attention(nq, gate, kvc, skv, wkv):
    b, t, w = nq.shape
    tq, tk = NSA_Q_TILE, NSA_K_TILE
    n_cmp = kvc.shape[1]
    assert t % tk == 0 and tk % tq == 0 and n_cmp == t // CMP_STRIDE and n_cmp % LANES == 0 and n_cmp <= tk
    k_spec = lambda rows: pl.BlockSpec((1, rows, KV_LANES), lambda bi, i: (bi, 0, 0))
    v_spec = lambda rows: pl.BlockSpec((1, rows, KV_LANES), lambda bi, i: (bi, 0, 1))
    return pl.pallas_call(
        _nsa_prompt_kernel,
        out_shape=jax.ShapeDtypeStruct((b, t, w), F32),
        grid=(b, t // tq),
        in_specs=[
            pl.BlockSpec((1, tq, w), lambda bi, i: (bi, i, 0)),
            pl.BlockSpec((1, tq, gate.shape[2]), lambda bi, i: (bi, i, 0)),
            k_spec(n_cmp), v_spec(n_cmp), k_spec(t), v_spec(t), k_spec(t), v_spec(t),
        ],
        out_specs=pl.BlockSpec((1, tq, w), lambda bi, i: (bi, i, 0)),
        scratch_shapes=[pltpu.VMEM((NSA_ROWS, KV_LANES), BF16), pltpu.VMEM((NSA_ROWS, tk), F32),
                        pltpu.VMEM((NSA_ROWS, tk), BF16), pltpu.VMEM((NSA_ROWS, LANES), F32),
                        pltpu.VMEM((NSA_ROWS, LANES), F32), pltpu.VMEM((NSA_ROWS, LANES), F32),
                        pltpu.VMEM((NSA_ROWS, KV_LANES), F32), pltpu.VMEM((t // SLC_LEN, tq), F32)],
        compiler_params=pltpu.CompilerParams(
            dimension_semantics=("arbitrary", "arbitrary"), vmem_limit_bytes=VMEM_LIMIT_BYTES),
        name="nsa_prompt_attention",
    )(nq, gate, kvc, kvc, skv, skv, wkv, wkv)


SSM_LANES = SSM_GROUPS * SSM_STATE
SSM_STEPS = 256


def _ssm_kernel(u_ref, h0_ref, a_ref, bb_ref, cc_ref, d_ref, glu_ref, o_ref, hlast_ref, h_scr, st_scr,
                *, n_seq, n_steps, seq_major):
    @pl.when(pl.program_id(0) == 0)
    def _():
        st_scr[...] = h0_ref[...]

    u = u_ref[...].reshape(n_seq * n_steps, SSM_WIDTH)
    h_scr[...] = jnp.dot(u.astype(BF16), bb_ref[...], preferred_element_type=F32)
    slab = 1 if seq_major else n_seq
    a_re = jnp.broadcast_to(a_ref[:, :SSM_LANES], (slab, SSM_LANES))
    a_im = jnp.broadcast_to(a_ref[:, SSM_LANES:], (slab, SSM_LANES))

    def advance(rows, h_re, h_im):
        bu = h_scr[rows, :]
        n_re = a_re * h_re - a_im * h_im + bu[:, :SSM_LANES]
        n_im = a_re * h_im + a_im * h_re + bu[:, SSM_LANES:]
        h_scr[rows, :SSM_LANES] = n_re
        h_scr[rows, SSM_LANES:] = n_im
        return n_re, n_im

    if seq_major:
        def body(t, carry):
            return tuple(advance(pl.ds(s * n_steps + t, 1), *carry[s]) for s in range(n_seq))
        init = tuple((st_scr[s:s + 1, :SSM_LANES], st_scr[s:s + 1, SSM_LANES:]) for s in range(n_seq))
        last = lax.fori_loop(0, n_steps, body, init)
        for s in range(n_seq):
            st_scr[s:s + 1, :SSM_LANES] = last[s][0]
            st_scr[s:s + 1, SSM_LANES:] = last[s][1]
    else:
        h_re, h_im = st_scr[:, :SSM_LANES], st_scr[:, SSM_LANES:]
        for t in range(n_steps):
            h_re, h_im = advance(slice(t * n_seq, (t + 1) * n_seq), h_re, h_im)
        st_scr[:, :SSM_LANES] = h_re
        st_scr[:, SSM_LANES:] = h_im
    hlast_ref[...] = st_scr[...]

    y = jnp.dot(h_scr[...].astype(BF16), cc_ref[...], preferred_element_type=F32) + d_ref[...] * u
    z = jnp.dot(y.astype(BF16), glu_ref[...], preferred_element_type=F32)
    o_ref[...] = (z[:, :SSM_WIDTH] * jax.nn.sigmoid(z[:, SSM_WIDTH:])).reshape(o_ref.shape)


def _ssm_weights(disc, c_re, c_im, d, glu_w):
    a_re, a_im, bb_re, bb_im = disc
    eye = jnp.eye(SSM_GROUPS, dtype=F32)
    b_in = jnp.concatenate([jnp.einsum('gpc,gx->gcxp', bb, eye).reshape(SSM_WIDTH, SSM_LANES)
                            for bb in (bb_re, bb_im)], axis=1).astype(BF16)
    c_out = jnp.concatenate([jnp.einsum('gcp,gx->gpxc', cc, eye).reshape(SSM_LANES, SSM_WIDTH)
                             for cc in (c_re.astype(F32), -c_im.astype(F32))], axis=0).astype(BF16)
    a = jnp.concatenate([a_re.reshape(1, SSM_LANES), a_im.reshape(1, SSM_LANES)], axis=1)
    return a, b_in, c_out, d.reshape(1, SSM_WIDTH).astype(F32), glu_w.astype(BF16)


def ssm_group(u, h0, weights):
    n_seq, t, w = u.shape
    a, b_in, c_out, d, glu = weights
    seq_major = n_seq % 8 != 0
    if seq_major:
        steps = min(SSM_STEPS, t)
        x, block, imap = u, (n_seq, steps, w), (lambda i: (0, i, 0))
    else:
        steps = t
        x, block, imap = jnp.transpose(u, (1, 0, 2)).reshape(t * n_seq, w), (t * n_seq, w), (lambda i: (0, 0))
    assert t % steps == 0
    rows = steps * n_seq
    const = lambda i: (0, 0)
    out, h_last = pl.pallas_call(
        functools.partial(_ssm_kernel, n_seq=n_seq, n_steps=steps, seq_major=seq_major),
        out_shape=(jax.ShapeDtypeStruct(x.shape, F32), jax.ShapeDtypeStruct((n_seq, 2 * SSM_LANES), F32)),
        grid=(t // steps,),
        in_specs=[
            pl.BlockSpec(block, imap),
            pl.BlockSpec((n_seq, 2 * SSM_LANES), const),
            pl.BlockSpec((1, 2 * SSM_LANES), const),
            pl.BlockSpec((SSM_WIDTH, 2 * SSM_LANES), const),
            pl.BlockSpec((2 * SSM_LANES, SSM_WIDTH), const),
            pl.BlockSpec((1, SSM_WIDTH), const),
            pl.BlockSpec((SSM_WIDTH, 2 * SSM_WIDTH), const),
        ],
        out_specs=(pl.BlockSpec(block, imap), pl.BlockSpec((n_seq, 2 * SSM_LANES), const)),
        scratch_shapes=[pltpu.VMEM((rows, 2 * SSM_LANES), F32), pltpu.VMEM((n_seq, 2 * SSM_LANES), F32)],
        compiler_params=pltpu.CompilerParams(
            dimension_semantics=("arbitrary",), vmem_limit_bytes=VMEM_LIMIT_BYTES),
        name="ssm_mix",
    )(x, h0, a, b_in, c_out, d, glu)
    if not seq_major:
        out = jnp.transpose(out.reshape(t, n_seq, w), (1, 0, 2))
    return out, h_last.reshape(n_seq, 2, SSM_GROUPS, SSM_STATE)


GATHER_PAGES_PER_STEP = 4


def _gather_pages_kernel(pt_ref, *refs):
    del pt_ref
    o_ref = refs[-1]
    page = refs[0].shape[1]
    for k, src in enumerate(refs[:-1]):
        o_ref[0, k * page:(k + 1) * page, :] = src[0]


def gather_pages(cache, page_table):
    n_seq, n_pages = page_table.shape
    page = cache.shape[1]
    width = math.prod(cache.shape[2:])
    per = GATHER_PAGES_PER_STEP
    assert n_pages % per == 0
    page_spec = lambda k: pl.BlockSpec((1, page, width), lambda b, j, pt: (pt[b, j * per + k], 0, 0))
    flat = cache.reshape(cache.shape[0], page, width)
    out = pl.pallas_call(
        _gather_pages_kernel,
        out_shape=jax.ShapeDtypeStruct((n_seq, n_pages * page, width), cache.dtype),
        grid_spec=pltpu.PrefetchScalarGridSpec(
            num_scalar_prefetch=1,
            grid=(n_seq, n_pages // per),
            in_specs=[page_spec(k) for k in range(per)],
            out_specs=pl.BlockSpec((1, per * page, width), lambda b, j, pt: (b, j, 0)),
        ),
        compiler_params=pltpu.CompilerParams(
            dimension_semantics=("arbitrary", "arbitrary"), vmem_limit_bytes=VMEM_LIMIT_BYTES),
        name="gather_pages",
    )(page_table, *([flat] * per))
    return out.reshape((n_seq, n_pages * page) + cache.shape[2:])


def rmsnorm(x, g):
    xf = x.astype(jnp.float32)
    y = xf * lax.rsqrt(jnp.mean(xf * xf, axis=-1, keepdims=True) + EPS)
    return (y * g.astype(jnp.float32)).astype(x.dtype)


def masked_softmax(s, mask):
    s = jnp.where(mask, s.astype(jnp.float32), -jnp.inf)
    m = jnp.max(s, axis=-1, keepdims=True)
    m = jnp.where(jnp.isfinite(m), m, 0.0)
    e = jnp.where(mask, jnp.exp(s - m), 0.0)
    return e / jnp.maximum(jnp.sum(e, axis=-1, keepdims=True), 1e-30)


def mixer_projection(x, norm_g, w_in):
    b, t, _ = x.shape
    sizes = (DIFF_HEADS * 2 * DIFF_QK, DIFF_HEADS * 2 * DIFF_QK, DIFF_WIDTH, NSA_WIDTH,
             NSA_KV_WIDTH, NSA_KV_WIDTH, NSA_KV_WIDTH, 3 * NSA_HEADS, SSM_WIDTH)
    offsets = np.cumsum(sizes)[:-1].tolist()
    dq, dk, dv, nq, ckv, skv, wkv, gate, u = jnp.split(rmsnorm(x, norm_g) @ w_in, offsets, axis=-1)
    kv5 = lambda z: z.reshape(b, t, 2, NSA_KV_HEADS, NSA_HD)
    dq = dq.reshape(b, t, DIFF_HEADS, 2, DIFF_QK)
    dkv = jnp.stack([dk.reshape(b, t, DIFF_HEADS, 2 * DIFF_QK), dv.reshape(b, t, DIFF_HEADS, DIFF_V)], axis=2)
    return dq, dkv, nq.reshape(b, t, NSA_HEADS, NSA_HD), kv5(ckv), kv5(skv), kv5(wkv), gate, u


def diff_core(q, k, v, qpos, kpos, lam):
    s = jnp.einsum('bqhcd,bkhcd->bhcqk', q, k) * (DIFF_QK ** -0.5)
    p = masked_softmax(s, kpos[None, :] <= qpos[:, None])
    a = p[:, :, 0] - lam * p[:, :, 1]
    return jnp.einsum('bhqk,bkhd->bqhd', a.astype(v.dtype), v)


def diff_prompt(dq, dkv, lam):
    b, t = dq.shape[:2]
    k = dkv[:, :, 0].reshape(b, t, DIFF_HEADS, 2, DIFF_QK)
    v = dkv[:, :, 1]
    kpos = jnp.arange(t)

    def block(i):
        s0 = i * Q_BLOCK
        qb = lax.dynamic_slice_in_dim(dq, s0, Q_BLOCK, axis=1)
        return diff_core(qb, k, v, s0 + jnp.arange(Q_BLOCK), kpos, lam)

    o = lax.map(block, jnp.arange(t // Q_BLOCK))
    return jnp.moveaxis(o, 0, 1).reshape(b, t, DIFF_HEADS, DIFF_V)


def diff_sample(dq, dkv, past_dkv, lam):
    b, s = dq.shape[:2]
    past_len = past_dkv.shape[1]
    kv = jnp.concatenate([past_dkv, dkv], axis=1)
    t = kv.shape[1]
    k = kv[:, :, 0].reshape(b, t, DIFF_HEADS, 2, DIFF_QK)
    return diff_core(dq, k, kv[:, :, 1], past_len + jnp.arange(s), jnp.arange(t), lam)


def diff_finish(o, gain, lam_init):
    b, t = o.shape[:2]
    return (rmsnorm(o, gain.reshape(DIFF_HEADS, DIFF_V)) * (1.0 - lam_init)).reshape(b, t, DIFF_WIDTH)


def nsa_compress(kv, pe, w1, w2):
    b, t = kv.shape[:2]
    r = CMP_LEN // CMP_STRIDE
    ns = t // CMP_STRIDE
    nc = ns - r + 1
    ch = kv[:, :ns * CMP_STRIDE].reshape(b, ns, CMP_STRIDE, 2, NSA_KV_HEADS, NSA_HD)
    blk = jnp.concatenate([ch[:, i:i + nc] for i in range(r)], axis=2)
    blk = blk + jnp.transpose(pe, (1, 0, 2))[:, :, None, :]
    flat = jnp.transpose(blk, (0, 1, 3, 4, 2, 5)).reshape(b, nc, 2, NSA_KV_HEADS, CMP_LEN * NSA_HD)
    hid = jax.nn.gelu(jnp.einsum('bnckf,cfe->bncke', flat, w1))
    out = jnp.einsum('bncke,ced->bnckd', hid, w2)
    cend = jnp.arange(nc) * CMP_STRIDE + CMP_LEN - 1
    return out[:, :, 0], out[:, :, 1], cend


def slc_blocks(kv):
    b, t = kv.shape[:2]
    nb = -(-t // SLC_LEN)
    kv = jnp.pad(kv, ((0, 0), (0, nb * SLC_LEN - t), (0, 0), (0, 0), (0, 0)))
    blk = jnp.transpose(kv.reshape(b, nb, SLC_LEN, 2, NSA_KV_HEADS, NSA_HD), (3, 0, 4, 1, 2, 5))
    return blk[0], blk[1]


def overlap_matrix(nc, nb):
    cs = jnp.arange(nc) * CMP_STRIDE
    js = jnp.arange(nb) * SLC_LEN
    return ((cs[:, None] < js[None, :] + SLC_LEN) & (cs[:, None] + CMP_LEN > js[None, :])).astype(jnp.float32)


def nsa_core(q, qpos, kc, vc, cend, ks, vs, kw, vw, kwpos, gate):
    b, nq = q.shape[:2]
    qg = q.reshape(b, nq, NSA_KV_HEADS, NSA_GROUP, NSA_HD)
    scale = NSA_HD ** -0.5
    s = jnp.einsum('bqkgd,bnkd->bqkgn', qg, kc) * scale
    p_c = masked_softmax(s, (cend[None, :] <= qpos[:, None])[None, :, None, None, :])
    o_c = jnp.einsum('bqkgn,bnkd->bqkgd', p_c.astype(vc.dtype), vc)
    nb = ks.shape[2]
    imp = jnp.einsum('bqkn,nj->bqkj', jnp.sum(p_c, axis=3), overlap_matrix(kc.shape[1], nb))
    j = jnp.arange(nb)[None, :]
    jt = (qpos // SLC_LEN)[:, None]
    forced = ((j == 0) | (j == jt) | (j == jt - 1))[None, :, None, :]
    score = jnp.where(forced, jnp.inf, jnp.where((j <= jt)[None, :, None, :], imp, -jnp.inf))
    n_sel = min(N_SEL, nb)
    _, idx = lax.top_k(score, n_sel)
    bi = jnp.arange(b)[:, None, None, None]
    hi = jnp.arange(NSA_KV_HEADS)[None, None, :, None]
    ksel = ks[bi, hi, idx]
    vsel = vs[bi, hi, idx]
    s = jnp.einsum('bqkgd,bqknld->bqkgnl', qg, ksel) * scale
    kpos = idx[..., None] * SLC_LEN + jnp.arange(SLC_LEN)
    smask = (kpos <= qpos[None, :, None, None, None]).reshape(b, nq, NSA_KV_HEADS, 1, n_sel * SLC_LEN)
    p_s = masked_softmax(s.reshape(b, nq, NSA_KV_HEADS, NSA_GROUP, n_sel * SLC_LEN), smask).reshape(s.shape)
    o_s = jnp.einsum('bqkgnl,bqknld->bqkgd', p_s.astype(vsel.dtype), vsel)
    s = jnp.einsum('bqkgd,bwkd->bqkgw', qg, kw) * scale
    wmask = (kwpos[None, :] <= qpos[:, None]) & (kwpos[None, :] > qpos[:, None] - WINDOW) & (kwpos[None, :] >= 0)
    p_w = masked_softmax(s, wmask[None, :, None, None, :])
    o_w = jnp.einsum('bqkgw,bwkd->bqkgd', p_w.astype(vw.dtype), vw)
    g = jax.nn.sigmoid(gate.astype(jnp.float32)).reshape(b, nq, NSA_KV_HEADS, NSA_GROUP, 3)
    o = g[..., 0:1] * o_c + g[..., 1:2] * o_s + g[..., 2:3] * o_w
    return o.reshape(b, nq, NSA_WIDTH).astype(q.dtype)


def nsa_prompt(nq, ckv, skv, wkv, gate, pe, w1, w2):
    b, t = nq.shape[:2]
    kc, vc, cend = nsa_compress(ckv, pe, w1, w2)
    ks, vs = slc_blocks(skv)
    wpad = jnp.pad(wkv, ((0, 0), (WINDOW, 0), (0, 0), (0, 0), (0, 0)))

    def block(i):
        s0 = i * Q_BLOCK
        qb = lax.dynamic_slice_in_dim(nq, s0, Q_BLOCK, axis=1)
        gb = lax.dynamic_slice_in_dim(gate, s0, Q_BLOCK, axis=1)
        wb = lax.dynamic_slice_in_dim(wpad, s0, WINDOW + Q_BLOCK, axis=1)
        qpos = s0 + jnp.arange(Q_BLOCK)
        kwpos = s0 - WINDOW + jnp.arange(WINDOW + Q_BLOCK)
        return nsa_core(qb, qpos, kc, vc, cend, ks, vs, wb[:, :, 0], wb[:, :, 1], kwpos, gb)

    o = lax.map(block, jnp.arange(t // Q_BLOCK))
    return jnp.moveaxis(o, 0, 1).reshape(b, t, NSA_WIDTH), wkv[:, -min(WINDOW, t):]


def nsa_sample(nq, ckv, skv, wkv, gate, past_ckv, past_skv, win_buf, past_len, pe, w1, w2):
    s = nq.shape[1]
    wb = win_buf.shape[1]
    kc, vc, cend = nsa_compress(jnp.concatenate([past_ckv, ckv], axis=1), pe, w1, w2)
    ks, vs = slc_blocks(jnp.concatenate([past_skv, skv], axis=1))
    wfull = jnp.concatenate([win_buf, wkv], axis=1)
    qpos = past_len + jnp.arange(s)
    kwpos = past_len - wb + jnp.arange(wb + s)
    o = nsa_core(nq, qpos, kc, vc, cend, ks, vs, wfull[:, :, 0], wfull[:, :, 1], kwpos, gate)
    return o, wfull[:, -min(WINDOW, wb + s):]


def ssm_discretize(lam_re, lam_im, log_dt, b_re, b_im):
    f32 = jnp.float32
    lr = jnp.minimum(lam_re.astype(f32), -1e-4)
    li = lam_im.astype(f32)
    dt = jnp.exp(log_dt.astype(f32))[:, None]
    mag = jnp.exp(lr * dt)
    a_re, a_im = mag * jnp.cos(li * dt), mag * jnp.sin(li * dt)
    den = lr * lr + li * li
    nr = a_re - 1.0
    f_re = (nr * lr + a_im * li) / den
    f_im = (a_im * lr - nr * li) / den
    b_re, b_im = b_re.astype(f32), b_im.astype(f32)
    bb_re = f_re[..., None] * b_re - f_im[..., None] * b_im
    bb_im = f_re[..., None] * b_im + f_im[..., None] * b_re
    return a_re, a_im, bb_re, bb_im


def ssm_combine(e1, e2):
    a1r, a1i, b1r, b1i = e1
    a2r, a2i, b2r, b2i = e2
    return (a2r * a1r - a2i * a1i, a2r * a1i + a2i * a1r,
            a2r * b1r - a2i * b1i + b2r, a2r * b1i + a2i * b1r + b2i)


def ssm_mixer(u, h0_re, h0_im, disc, c_re, c_im, d, glu_w):
    f32 = jnp.float32
    a_re, a_im, bb_re, bb_im = disc
    b, t, _ = u.shape
    uf = u.astype(f32)
    ug = uf.reshape(b, t, SSM_GROUPS, SSM_CH)
    bu_re = jnp.einsum('gpc,btgc->btgp', bb_re, ug)
    bu_im = jnp.einsum('gpc,btgc->btgp', bb_im, ug)
    h0_re, h0_im = h0_re.astype(f32), h0_im.astype(f32)
    bu_re = bu_re.at[:, 0].add(a_re * h0_re - a_im * h0_im)
    bu_im = bu_im.at[:, 0].add(a_re * h0_im + a_im * h0_re)
    ar = jnp.broadcast_to(a_re, bu_re.shape)
    ai = jnp.broadcast_to(a_im, bu_im.shape)
    _, _, h_re, h_im = lax.associative_scan(ssm_combine, (ar, ai, bu_re, bu_im), axis=1)
    y = jnp.einsum('gcp,btgp->btgc', c_re.astype(f32), h_re) - jnp.einsum('gcp,btgp->btgc', c_im.astype(f32), h_im)
    y = y.reshape(b, t, SSM_WIDTH) + d.astype(f32) * uf
    z = y @ glu_w.astype(f32)
    out = z[..., :SSM_WIDTH] * jax.nn.sigmoid(z[..., SSM_WIDTH:])
    return out.astype(u.dtype), jnp.stack([h_re[:, -1], h_im[:, -1]], axis=1).astype(u.dtype)


def kernel(x_prompt, x_sample, cache_diff_kv, cache_nsa_cmp_kv, cache_nsa_slc_kv, state_nsa_win_kv, state_ssm, page_table, ffn1_norm, ffn1_w_gate, ffn1_w_up, ffn1_w_down, mix_norm, w_in, diff_lambda, diff_head_norm, nsa_cmp_pe, nsa_cmp_w1, nsa_cmp_w2, ssm_lambda_re, ssm_lambda_im, ssm_log_dt, ssm_b_re, ssm_b_im, ssm_c_re, ssm_c_im, ssm_d, ssm_glu_w, w_out, ffn2_norm, ffn2_w_gate, ffn2_w_up, ffn2_w_down, final_norm):
    f32 = jnp.float32
    n_pages = page_table.shape[1]
    past_len = n_pages * cache_diff_kv.shape[2]
    b_p, t_p, d = x_prompt.shape
    b_s, t_s, _ = x_sample.shape
    n_p = b_p * t_p
    x = jnp.concatenate([x_prompt.reshape(n_p, d), x_sample.reshape(b_s * t_s, d)], axis=0)
    dkv_p, dkv_s, ckv_p, ckv_s, skv_p, skv_s, win_p, win_s, ssm_p, ssm_s = ([] for _ in range(10))
    for l in range(DEPTH):
        lam_init = 0.8 - 0.6 * math.exp(-0.3 * l)
        lv = diff_lambda[l].astype(f32)
        lam = jnp.exp(jnp.sum(lv[0] * lv[1])) - jnp.exp(jnp.sum(lv[2] * lv[3])) + lam_init
        disc = ssm_discretize(ssm_lambda_re[l], ssm_lambda_im[l], ssm_log_dt[l], ssm_b_re[l], ssm_b_im[l])
        ssm_w = (ssm_c_re[l], ssm_c_im[l], ssm_d[l], ssm_glu_w[l])
        cmp_w = (nsa_cmp_pe[l], nsa_cmp_w1[l], nsa_cmp_w2[l])

        x = ffn_half(x, ffn1_norm[l], ffn1_w_gate[l], ffn1_w_up[l], ffn1_w_down[l])
        xp = x[:n_p].reshape(b_p, t_p, d)
        xs = x[n_p:].reshape(b_s, t_s, d)

        dq, dkv, nq, ckv, skv, wkv, gate, u = mixer_projection(xp, mix_norm[l], w_in[l])
        o_d = diff_prompt_attention(dq.reshape(b_p, t_p, DIFF_WIDTH), dkv[:, :, 0].reshape(b_p, t_p, DIFF_WIDTH),
                                    dkv[:, :, 1].reshape(b_p, t_p, DIFF_WIDTH), lam, diff_head_norm[l], lam_init)
        cmp_big = _compress_weights(*cmp_w)
        kvc = nsa_compress_all(ckv.reshape(b_p, t_p, NSA_KV_WIDTH), cmp_big, t_p // CMP_STRIDE)
        o_n = nsa_prompt_attention(nq.reshape(b_p, t_p, NSA_WIDTH), gate, kvc,
                                   skv.reshape(b_p, t_p, NSA_KV_WIDTH).astype(BF16),
                                   wkv.reshape(b_p, t_p, NSA_KV_WIDTH).astype(BF16))
        win = wkv[:, -min(WINDOW, t_p):]
        ssm_big = _ssm_weights(disc, *ssm_w)
        o_s, h = ssm_group(u, jnp.zeros((b_p, 2 * SSM_LANES), f32), ssm_big)
        xp = xp + jnp.concatenate([o_d, o_n, o_s], axis=-1) @ w_out[l]
        dkv_p.append(dkv); ckv_p.append(ckv); skv_p.append(skv); win_p.append(win); ssm_p.append(h)

        dq, dkv, nq, ckv, skv, wkv, gate, u = mixer_projection(xs, mix_norm[l], w_in[l])
        o_d = diff_sample(dq, dkv, gather_pages(cache_diff_kv[l], page_table), lam)
        o_n, win = nsa_sample(nq, ckv, skv, wkv, gate, gather_pages(cache_nsa_cmp_kv[l], page_table),
                              gather_pages(cache_nsa_slc_kv[l], page_table), state_nsa_win_kv[l], past_len, *cmp_w)
        o_s, h = ssm_group(u, state_ssm[l].reshape(b_s, 2 * SSM_LANES), ssm_big)
        xs = xs + jnp.concatenate([diff_finish(o_d, diff_head_norm[l], lam_init), o_n, o_s], axis=-1) @ w_out[l]
        dkv_s.append(dkv); ckv_s.append(ckv); skv_s.append(skv); win_s.append(win); ssm_s.append(h)

        x = jnp.concatenate([xp.reshape(n_p, d), xs.reshape(b_s * t_s, d)], axis=0)
        x = ffn_half(x, ffn2_norm[l], ffn2_w_gate[l], ffn2_w_up[l], ffn2_w_down[l])

    y = rmsnorm(x, final_norm)
    y_prompt = y[:n_p].reshape(b_p, t_p, d)
    y_sample = y[n_p:].reshape(b_s, t_s, d)
    return (y_prompt, y_sample, jnp.stack(dkv_p), jnp.stack(dkv_s), jnp.stack(ckv_p), jnp.stack(ckv_s),
            jnp.stack(skv_p), jnp.stack(skv_s), jnp.stack(win_p), jnp.stack(win_s), jnp.stack(ssm_p), jnp.stack(ssm_s))
```

```python
import functools
import math

import jax
import jax.numpy as jnp
import numpy as np
from jax import lax
from jax.experimental import pallas as pl
from jax.experimental.pallas import tpu as pltpu

D_MODEL = 1024
DEPTH = 4
DIFF_HEADS = 4
DIFF_QK = 32
DIFF_V = 64
DIFF_WIDTH = DIFF_HEADS * DIFF_V
NSA_HEADS = 6
NSA_KV_HEADS = 2
NSA_GROUP = NSA_HEADS // NSA_KV_HEADS
NSA_HD = 64
NSA_WIDTH = NSA_HEADS * NSA_HD
NSA_KV_WIDTH = 2 * NSA_KV_HEADS * NSA_HD
CMP_LEN = 32
CMP_STRIDE = 16
SLC_LEN = 64
N_SEL = 16
WINDOW = 512
SSM_GROUPS = 24
SSM_CH = 16
SSM_STATE = 64
SSM_WIDTH = SSM_GROUPS * SSM_CH
MIX_WIDTH = DIFF_WIDTH + NSA_WIDTH + SSM_WIDTH
D_FF = 2816
Q_BLOCK = 128
EPS = 1e-5

VMEM_LIMIT_BYTES = 56 * 1024 * 1024
MXU_WIDTH = 256

F32 = jnp.float32
BF16 = jnp.bfloat16


def _rms_rows(x, g):
    return x * lax.rsqrt(jnp.mean(x * x, axis=-1, keepdims=True) + EPS) * g


FFN_TOKENS = 512
FFN_CHUNK = MXU_WIDTH


def _ffn_kernel(x_ref, g_ref, wg_ref, wu_ref, wd_ref, o_ref, h_scr, acc_scr):
    x = x_ref[...]
    h_scr[...] = _rms_rows(x, g_ref[...]).astype(BF16)
    acc_scr[...] = jnp.zeros_like(acc_scr)

    def chunk(j, carry):
        h = h_scr[...]
        gate = jnp.dot(h, wg_ref[j], preferred_element_type=F32)
        up = jnp.dot(h, wu_ref[j], preferred_element_type=F32)
        act = (gate * jax.nn.sigmoid(gate) * up).astype(BF16)
        acc_scr[...] += jnp.dot(act, wd_ref[j], preferred_element_type=F32)
        return carry

    lax.fori_loop(0, wg_ref.shape[0], chunk, 0)
    o_ref[...] = x + 0.5 * acc_scr[...]


def ffn_half(x, g, w_gate, w_up, w_down):
    n_tok, d = x.shape
    d_ff = w_gate.shape[1]
    n_chunk = d_ff // FFN_CHUNK
    assert n_tok % FFN_TOKENS == 0 and d_ff % FFN_CHUNK == 0
    wg = w_gate.astype(BF16).reshape(d, n_chunk, FFN_CHUNK).transpose(1, 0, 2)
    wu = w_up.astype(BF16).reshape(d, n_chunk, FFN_CHUNK).transpose(1, 0, 2)
    wd = w_down.astype(BF16).reshape(n_chunk, FFN_CHUNK, d)
    const3 = lambda i: (0, 0, 0)
    return pl.pallas_call(
        _ffn_kernel,
        out_shape=jax.ShapeDtypeStruct((n_tok, d), F32),
        grid=(n_tok // FFN_TOKENS,),
        in_specs=[
            pl.BlockSpec((FFN_TOKENS, d), lambda i: (i, 0)),
            pl.BlockSpec((1, d), lambda i: (0, 0)),
            pl.BlockSpec((n_chunk, d, FFN_CHUNK), const3, pipeline_mode=pl.Buffered(1)),
            pl.BlockSpec((n_chunk, d, FFN_CHUNK), const3, pipeline_mode=pl.Buffered(1)),
            pl.BlockSpec((n_chunk, FFN_CHUNK, d), const3, pipeline_mode=pl.Buffered(1)),
        ],
        out_specs=pl.BlockSpec((FFN_TOKENS, d), lambda i: (i, 0)),
        scratch_shapes=[pltpu.VMEM((FFN_TOKENS, d), BF16), pltpu.VMEM((FFN_TOKENS, d), F32)],
        compiler_params=pltpu.CompilerParams(
            dimension_semantics=("arbitrary",), vmem_limit_bytes=VMEM_LIMIT_BYTES),
        name="ffn_half",
    )(x, g.reshape(1, d), wg, wu, wd)


LANES = 128
NEG_INIT = -1e30
LOG2E = math.log2(math.e)


def _softmax_slabs(s_scr, p_scr, m_scr, l_scr, a_scr, n_slabs, tq, tk, mask_fn):
    n_tiles = tk // LANES
    for r in range(n_slabs):
        rows = slice(r * tq, (r + 1) * tq)
        tiles = []
        for j in range(n_tiles):
            t = s_scr[rows, j * LANES:(j + 1) * LANES]
            if mask_fn is not None:
                t = jnp.where(mask_fn(r, j), t, -jnp.inf)
            tiles.append(t)
        m_old = m_scr[rows, :]
        row_max = jnp.max(functools.reduce(jnp.maximum, tiles), axis=1, keepdims=True)
        m_new = jnp.maximum(m_old, row_max)
        alpha = jnp.exp2(m_old - m_new)
        probs = [jnp.exp2(t - m_new) for t in tiles]
        l_scr[rows, :] = alpha * l_scr[rows, :] + functools.reduce(jnp.add, probs)
        m_scr[rows, :] = m_new
        a_scr[rows, :] = alpha
        for j in range(n_tiles):
            p_scr[rows, j * LANES:(j + 1) * LANES] = probs[j].astype(p_scr.dtype)


def _accumulate(acc_scr, a_scr, p_scr, v):
    pv = jnp.dot(p_scr[...].astype(BF16), v, preferred_element_type=F32)
    alpha = a_scr[...]
    for c in range(acc_scr.shape[1] // LANES):
        cols = slice(c * LANES, (c + 1) * LANES)
        acc_scr[:, cols] = alpha * acc_scr[:, cols] + pv[:, cols]


def _reset_softmax(m_scr, l_scr, acc_scr):
    m_scr[...] = jnp.full_like(m_scr, NEG_INIT)
    l_scr[...] = jnp.zeros_like(l_scr)
    acc_scr[...] = jnp.zeros_like(acc_scr)


DIFF_Q_TILE = 128
DIFF_K_TILE = 512
N_DIFF_MAPS = 2 * DIFF_HEADS


def _diff_finish(acc_scr, l_scr, lam, gain, out_scale, tq):
    lane_o = lax.broadcasted_iota(jnp.int32, (tq, DIFF_WIDTH), 1)
    o = jnp.zeros((tq, DIFF_WIDTH), F32)
    inv_n = jnp.zeros((tq, DIFF_WIDTH), F32)
    for h in range(DIFF_HEADS):
        parts = []
        for c in range(2):
            rows = slice((2 * h + c) * tq, (2 * h + c + 1) * tq)
            l = jnp.maximum(jnp.sum(l_scr[rows, :], axis=1, keepdims=True), 1e-30)
            parts.append(acc_scr[rows, :] / l)
        in_head = (lane_o >= h * DIFF_V) & (lane_o < (h + 1) * DIFF_V)
        a = jnp.where(in_head, parts[0] - lam * parts[1], 0.0)
        ms = jnp.sum(a * a, axis=1, keepdims=True) * (1.0 / DIFF_V)
        o = o + a
        inv_n = jnp.where(in_head, lax.rsqrt(ms + EPS), inv_n)
    return o * inv_n * gain * out_scale


def _diff_prompt_kernel(lam_ref, dq_ref, dk_ref, dv_ref, gain_ref, o_ref,
                        q_scr, s_scr, p_scr, m_scr, l_scr, a_scr, acc_scr, *, out_scale):
    tq, tk = DIFF_Q_TILE, DIFF_K_TILE
    s0 = pl.program_id(1) * tq
    q = dq_ref[0] * (DIFF_QK ** -0.5 * LOG2E)
    lane = lax.broadcasted_iota(jnp.int32, q.shape, 1)
    for r in range(N_DIFF_MAPS):
        keep = (lane >= r * DIFF_QK) & (lane < (r + 1) * DIFF_QK)
        q_scr[r * tq:(r + 1) * tq, :] = jnp.where(keep, q, 0.0).astype(BF16)
    _reset_softmax(m_scr, l_scr, acc_scr)

    def step(kt, masked):
        k0 = pl.multiple_of(kt * tk, tk)
        s_scr[...] = lax.dot_general(q_scr[...], dk_ref[0, pl.ds(k0, tk), :], (((1,), (1,)), ((), ())),
                                     preferred_element_type=F32)
        mask_fn = None
        if masked:
            qpos = s0 + lax.broadcasted_iota(jnp.int32, (tq, LANES), 0)
            kpos = k0 + lax.broadcasted_iota(jnp.int32, (tq, LANES), 1)
            mask_fn = lambda r, j: kpos + j * LANES <= qpos
        _softmax_slabs(s_scr, p_scr, m_scr, l_scr, a_scr, N_DIFF_MAPS, tq, tk, mask_fn)
        _accumulate(acc_scr, a_scr, p_scr, dv_ref[0, pl.ds(k0, tk), :])

    kt_last = (s0 + tq - 1) // tk

    def body(kt, carry):
        step(kt, False)
        return carry

    lax.fori_loop(0, kt_last, body, 0)
    step(kt_last, True)
    o_ref[0] = _diff_finish(acc_scr, l_scr, lam_ref[0], gain_ref[...], out_scale, tq)


def diff_prompt_attention(dq, dk, dv, lam, gain, lam_init):
    b, t, w = dq.shape
    tq, tk = DIFF_Q_TILE, DIFF_K_TILE
    assert t % tk == 0 and tk % tq == 0
    rows = N_DIFF_MAPS * tq
    return pl.pallas_call(
        functools.partial(_diff_prompt_kernel, out_scale=1.0 - lam_init),
        out_shape=jax.ShapeDtypeStruct((b, t, w), F32),
        grid=(b, t // tq),
        in_specs=[
            pl.BlockSpec(memory_space=pltpu.SMEM),
            pl.BlockSpec((1, tq, w), lambda bi, i: (bi, i, 0)),
            pl.BlockSpec((1, t, w), lambda bi, i: (bi, 0, 0)),
            pl.BlockSpec((1, t, w), lambda bi, i: (bi, 0, 0)),
            pl.BlockSpec((1, w), lambda bi, i: (0, 0)),
        ],
        out_specs=pl.BlockSpec((1, tq, w), lambda bi, i: (bi, i, 0)),
        scratch_shapes=[pltpu.VMEM((rows, w), BF16), pltpu.VMEM((rows, tk), F32), pltpu.VMEM((rows, tk), BF16),
                        pltpu.VMEM((rows, LANES), F32), pltpu.VMEM((rows, LANES), F32),
                        pltpu.VMEM((rows, LANES), F32), pltpu.VMEM((rows, w), F32)],
        compiler_params=pltpu.CompilerParams(
            dimension_semantics=("arbitrary", "arbitrary"), vmem_limit_bytes=VMEM_LIMIT_BYTES),
        name="diff_prompt_attention",
    )(lam.reshape(1), dq, dk.astype(BF16), dv.astype(BF16), gain.reshape(1, w))


CMP_CHUNK_WIDTH = CMP_STRIDE * NSA_KV_WIDTH


def _compress_rows(x, pe_a_ref, pe_b_ref, w1a_ref, w1b_ref, w2_ref):
    rows = x.shape[0]
    first = jnp.dot((x + pe_a_ref[...]).astype(BF16), w1a_ref[...], preferred_element_type=F32)
    second = jnp.dot((x + pe_b_ref[...]).astype(BF16), w1b_ref[...], preferred_element_type=F32)
    hid = jax.nn.gelu(first + pltpu.roll(second, rows - 1, 0))
    return jnp.dot(hid.astype(BF16), w2_ref[...], preferred_element_type=F32)


def _compress_kernel(x_ref, pe_a_ref, pe_b_ref, w1a_ref, w1b_ref, w2_ref, o_ref):
    o_ref[...] = _compress_rows(x_ref[...], pe_a_ref, pe_b_ref, w1a_ref, w1b_ref, w2_ref).astype(o_ref.dtype)


def _compress_weights(pe, w1, w2):
    eye = jnp.eye(2, dtype=F32)
    w1r = w1.reshape(2, CMP_LEN, NSA_HD, NSA_HD)
    big1 = jnp.einsum('clde,cx,ky->lckdxye', w1r, eye, eye).reshape(CMP_LEN, NSA_KV_WIDTH, NSA_KV_WIDTH)
    big1 = big1.astype(BF16)
    w1a = big1[:CMP_STRIDE].reshape(CMP_CHUNK_WIDTH, NSA_KV_WIDTH)
    w1b = big1[CMP_STRIDE:].reshape(CMP_CHUNK_WIDTH, NSA_KV_WIDTH)
    big2 = jnp.einsum('ced,cx,ky->ckexyd', w2, eye, eye).reshape(NSA_KV_WIDTH, NSA_KV_WIDTH).astype(BF16)
    pe16 = jnp.broadcast_to(jnp.transpose(pe, (1, 0, 2))[:, :, None, :], (CMP_LEN, 2, NSA_KV_HEADS, NSA_HD))
    pe_a = pe16[:CMP_STRIDE].reshape(1, CMP_CHUNK_WIDTH)
    pe_b = pe16[CMP_STRIDE:].reshape(1, CMP_CHUNK_WIDTH)
    return pe_a, pe_b, w1a, w1b, big2


def nsa_compress_all(kv, cmp_weights, rows_per_step):
    b, t, w = kv.shape
    ns = t // CMP_STRIDE
    assert t % CMP_STRIDE == 0 and rows_per_step % ns == 0 and (b * ns) % rows_per_step == 0
    x = kv.reshape(b * ns, CMP_CHUNK_WIDTH)
    pe_a, pe_b, w1a, w1b, w2 = cmp_weights
    const = lambda i: (0, 0)
    out = pl.pallas_call(
        _compress_kernel,
        out_shape=jax.ShapeDtypeStruct((b * ns, w), BF16),
        grid=(b * ns // rows_per_step,),
        in_specs=[
            pl.BlockSpec((rows_per_step, CMP_CHUNK_WIDTH), lambda i: (i, 0)),
            pl.BlockSpec((1, CMP_CHUNK_WIDTH), const), pl.BlockSpec((1, CMP_CHUNK_WIDTH), const),
            pl.BlockSpec((CMP_CHUNK_WIDTH, w), const), pl.BlockSpec((CMP_CHUNK_WIDTH, w), const),
            pl.BlockSpec((w, w), const),
        ],
        out_specs=pl.BlockSpec((rows_per_step, w), lambda i: (i, 0)),
        compiler_params=pltpu.CompilerParams(
            dimension_semantics=("arbitrary",), vmem_limit_bytes=VMEM_LIMIT_BYTES),
        name="nsa_compress",
    )(x, pe_a, pe_b, w1a, w1b, w2)
    return out.reshape(b, ns, w)


NSA_Q_TILE = 128
NSA_K_TILE = 512
NSA_ROWS = NSA_GROUP * NSA_Q_TILE
KV_LANES = NSA_KV_HEADS * NSA_HD
SLC_SHIFT = SLC_LEN.bit_length() - 1
assert 1 << SLC_SHIFT == SLC_LEN


def _split3(x):
    hi = x.astype(BF16)
    r1 = x - hi.astype(F32)
    mid = r1.astype(BF16)
    lo = (r1 - mid.astype(F32)).astype(BF16)
    return hi, mid, lo


def _selection_mask(score_t, st_scr, n_rows):
    st_scr[...] = score_t
    jidx = lax.broadcasted_iota(jnp.int32, score_t.shape, 0)

    def body(jp, cnt):
        row = jnp.broadcast_to(st_scr[pl.ds(jp, 1), :], score_t.shape)
        tie = jnp.where(jp < jidx, 1.0, 0.0)
        return cnt + jnp.where(row > score_t, 1.0, jnp.where(row == score_t, tie, 0.0))

    cnt = lax.fori_loop(0, n_rows, body, jnp.zeros(score_t.shape, F32))
    return jnp.where(cnt < float(N_SEL), 1.0, 0.0)


def _nsa_prompt_kernel(q_ref, gate_ref, kc_ref, vc_ref, ks_ref, vs_ref, kw_ref, vw_ref, o_ref,
                       q_scr, s_scr, p_scr, m_scr, l_scr, a_scr, acc_scr, st_scr):
    tq, tk = NSA_Q_TILE, NSA_K_TILE
    n_cmp = kc_ref.shape[1]
    n_blk = ks_ref.shape[1] // SLC_LEN
    s0 = pl.program_id(1) * tq
    qpos = s0 + lax.broadcasted_iota(jnp.int32, (tq, LANES), 0)
    lane = lax.broadcasted_iota(jnp.int32, (tq, LANES), 1)
    gates = jax.nn.sigmoid(gate_ref[0])
    kt_last = (s0 + tq - 1) // tk

    def finish():
        l = jnp.maximum(jnp.sum(l_scr[...], axis=1, keepdims=True), 1e-30)
        return acc_scr[...] / l

    for kvh in range(NSA_KV_HEADS):
        half = slice(kvh * NSA_HD, (kvh + 1) * NSA_HD)
        for g in range(NSA_GROUP):
            h = kvh * NSA_GROUP + g
            qh = q_ref[0, :, h * NSA_HD:(h + 1) * NSA_HD] * (NSA_HD ** -0.5 * LOG2E)
            zero = jnp.zeros_like(qh)
            qpad = jnp.concatenate([qh, zero] if kvh == 0 else [zero, qh], axis=1)
            q_scr[g * tq:(g + 1) * tq, :] = qpad.astype(BF16)

        n_ct = n_cmp // LANES
        s_scr[:, :n_cmp] = lax.dot_general(q_scr[...], kc_ref[0], (((1,), (1,)), ((), ())),
                                           preferred_element_type=F32)
        psum = [jnp.zeros((tq, LANES), F32) for _ in range(n_ct)]
        for g in range(NSA_GROUP):
            rows = slice(g * tq, (g + 1) * tq)
            masks = [(lane + j * LANES) * CMP_STRIDE + (CMP_LEN - 1) <= qpos for j in range(n_ct)]
            tiles = [jnp.where(masks[j], s_scr[rows, j * LANES:(j + 1) * LANES], -jnp.inf) for j in range(n_ct)]
            mx = jnp.max(functools.reduce(jnp.maximum, tiles), axis=1, keepdims=True)
            mx = jnp.where(mx == -jnp.inf, 0.0, mx)
            es = [jnp.where(masks[j], jnp.exp2(tiles[j] - mx), 0.0) for j in range(n_ct)]
            denom = jnp.maximum(jnp.sum(functools.reduce(jnp.add, es), axis=1, keepdims=True), 1e-30)
            for j in range(n_ct):
                pj = es[j] / denom
                psum[j] = psum[j] + pj
                p_scr[rows, j * LANES:(j + 1) * LANES] = pj.astype(BF16)
        o_cmp = jnp.dot(p_scr[:, :n_cmp], vc_ref[0], preferred_element_type=F32)

        cblk = lax.broadcasted_iota(jnp.int32, (n_cmp, n_blk), 0) * CMP_STRIDE
        sblk = lax.broadcasted_iota(jnp.int32, (n_cmp, n_blk), 1) * SLC_LEN
        overlap = jnp.where((cblk < sblk + SLC_LEN) & (cblk + CMP_LEN > sblk), 1.0, 0.0).astype(BF16)
        ps = jnp.concatenate(psum, axis=1)
        imp = sum(jnp.dot(t, overlap, preferred_element_type=F32) for t in _split3(ps))
        blk = lax.broadcasted_iota(jnp.int32, (tq, n_blk), 1)
        jt = jnp.right_shift(s0 + lax.broadcasted_iota(jnp.int32, (tq, n_blk), 0), SLC_SHIFT)
        forced = (blk == 0) | (blk == jt) | (blk == jt - 1)
        score = jnp.where(forced, jnp.inf, jnp.where(blk <= jt, imp, -jnp.inf))
        n_valid = (s0 + tq - 1) // SLC_LEN + 1
        sel_t = _selection_mask(score.T, st_scr, n_valid)
        sel = sel_t.T.astype(BF16)

        _reset_softmax(m_scr, l_scr, acc_scr)

        def slc_step(kt, causal):
            k0 = pl.multiple_of(kt * tk, tk)
            s_scr[...] = lax.dot_general(q_scr[...], ks_ref[0, pl.ds(k0, tk), :], (((1,), (1,)), ((), ())),
                                         preferred_element_type=F32)
            erow = lax.broadcasted_iota(jnp.int32, (n_blk, tk), 0)
            ecol = jnp.right_shift(k0 + lax.broadcasted_iota(jnp.int32, (n_blk, tk), 1), SLC_SHIFT)
            expand = jnp.where(erow == ecol, 1.0, 0.0).astype(BF16)
            member = jnp.dot(sel, expand, preferred_element_type=F32)

            def mask_fn(r, j):
                msk = member[:, j * LANES:(j + 1) * LANES] > 0.5
                if causal:
                    msk = msk & (k0 + j * LANES + lane <= qpos)
                return msk

            _softmax_slabs(s_scr, p_scr, m_scr, l_scr, a_scr, NSA_GROUP, tq, tk, mask_fn)
            _accumulate(acc_scr, a_scr, p_scr, vs_ref[0, pl.ds(k0, tk), :])

        def slc_body(kt, carry):
            slc_step(kt, False)
            return carry

        lax.fori_loop(0, kt_last, slc_body, 0)
        slc_step(kt_last, True)
        o_slc = finish()

        _reset_softmax(m_scr, l_scr, acc_scr)

        def win_step(kt):
            k0 = pl.multiple_of(kt * tk, tk)
            s_scr[...] = lax.dot_general(q_scr[...], kw_ref[0, pl.ds(k0, tk), :], (((1,), (1,)), ((), ())),
                                         preferred_element_type=F32)

            def mask_fn(r, j):
                kpos = k0 + j * LANES + lane
                return (kpos <= qpos) & (kpos > qpos - WINDOW)

            _softmax_slabs(s_scr, p_scr, m_scr, l_scr, a_scr, NSA_GROUP, tq, tk, mask_fn)
            _accumulate(acc_scr, a_scr, p_scr, vw_ref[0, pl.ds(k0, tk), :])

        def win_body(kt, carry):
            win_step(kt)
            return carry

        lax.fori_loop(jnp.maximum(s0 - WINDOW + 1, 0) // tk, kt_last + 1, win_body, 0)
        o_win = finish()

        for g in range(NSA_GROUP):
            rows = slice(g * tq, (g + 1) * tq)
            c = (kvh * NSA_GROUP + g) * 3
            mix = (gates[:, c:c + 1] * o_cmp[rows, half] + gates[:, c + 1:c + 2] * o_slc[rows, half]
                   + gates[:, c + 2:c + 3] * o_win[rows, half])
            h = kvh * NSA_GROUP + g
            o_ref[0, :, h * NSA_HD:(h + 1) * NSA_HD] = mix


def nsa_prompt_attention(nq, gate, kvc, skv, wkv):
    b, t, w = nq.shape
    tq, tk = NSA_Q_TILE, NSA_K_TILE
    n_cmp = kvc.shape[1]
    assert t % tk == 0 and tk % tq == 0 and n_cmp == t // CMP_STRIDE and n_cmp % LANES == 0 and n_cmp <= tk
    k_spec = lambda rows: pl.BlockSpec((1, rows, KV_LANES), lambda bi, i: (bi, 0, 0))
    v_spec = lambda rows: pl.BlockSpec((1, rows, KV_LANES), lambda bi, i: (bi, 0, 1))
    return pl.pallas_call(
        _nsa_prompt_kernel,
        out_shape=jax.ShapeDtypeStruct((b, t, w), F32),
        grid=(b, t // tq),
        in_specs=[
            pl.BlockSpec((1, tq, w), lambda bi, i: (bi, i, 0)),
            pl.BlockSpec((1, tq, gate.shape[2]), lambda bi, i: (bi, i, 0)),
            k_spec(n_cmp), v_spec(n_cmp), k_spec(t), v_spec(t), k_spec(t), v_spec(t),
        ],
        out_specs=pl.BlockSpec((1, tq, w), lambda bi, i: (bi, i, 0)),
        scratch_shapes=[pltpu.VMEM((NSA_ROWS, KV_LANES), BF16), pltpu.VMEM((NSA_ROWS, tk), F32),
                        pltpu.VMEM((NSA_ROWS, tk), BF16), pltpu.VMEM((NSA_ROWS, LANES), F32),
                        pltpu.VMEM((NSA_ROWS, LANES), F32), pltpu.VMEM((NSA_ROWS, LANES), F32),
                        pltpu.VMEM((NSA_ROWS, KV_LANES), F32), pltpu.VMEM((t // SLC_LEN, tq), F32)],
        compiler_params=pltpu.CompilerParams(
            dimension_semantics=("arbitrary", "arbitrary"), vmem_limit_bytes=VMEM_LIMIT_BYTES),
        name="nsa_prompt_attention",
    )(nq, gate, kvc, kvc, skv, skv, wkv, wkv)


SAMPLE_PAGES_PER_STEP = 4


def _pad_rows(x, rows):
    return jnp.concatenate([x, jnp.zeros((rows - x.shape[0], x.shape[1]), x.dtype)], axis=0)


def _diff_sample_kernel(pt_ref, lam_ref, dq_ref, kvn_ref, gain_ref, *refs, out_scale, n_new):
    del pt_ref
    per = SAMPLE_PAGES_PER_STEP
    page_refs, o_ref = refs[:per], refs[per]
    q_scr, s_scr, p_scr, m_scr, l_scr, a_scr, acc_scr, kv_scr = refs[per + 1:]
    page = page_refs[0].shape[2]
    rows = N_DIFF_MAPS * n_new
    j = pl.program_id(1)

    @pl.when(j == 0)
    def _():
        q = dq_ref[0] * (DIFF_QK ** -0.5 * LOG2E)
        lane = lax.broadcasted_iota(jnp.int32, q.shape, 1)
        for r in range(N_DIFF_MAPS):
            keep = (lane >= r * DIFF_QK) & (lane < (r + 1) * DIFF_QK)
            q_scr[r * n_new:(r + 1) * n_new, :] = jnp.where(keep, q, 0.0)
        _reset_softmax(m_scr, l_scr, acc_scr)

    for k in range(per):
        kv_scr[k * page:(k + 1) * page, :] = page_refs[k][0, 0].astype(BF16)
    qb = q_scr[...].astype(BF16)
    nt = (((1,), (1,)), ((), ()))
    s_scr[...] = lax.dot_general(qb, kv_scr[:, :DIFF_WIDTH], nt, preferred_element_type=F32)
    _softmax_slabs(s_scr, p_scr, m_scr, l_scr, a_scr, 1, rows, per * page, None)
    _accumulate(acc_scr, a_scr, p_scr, kv_scr[:, DIFF_WIDTH:])

    @pl.when(j == pl.num_programs(1) - 1)
    def _():
        kvn = _pad_rows(kvn_ref[0], LANES).astype(BF16)
        s_scr[:, :LANES] = lax.dot_general(qb, kvn[:, :DIFF_WIDTH], nt, preferred_element_type=F32)
        qi = lax.broadcasted_iota(jnp.int32, (rows, LANES), 0) & (n_new - 1)
        kj = lax.broadcasted_iota(jnp.int32, (rows, LANES), 1)
        _softmax_slabs(s_scr, p_scr, m_scr, l_scr, a_scr, 1, rows, LANES, lambda r, t: kj <= qi)
        _accumulate(acc_scr, a_scr, p_scr.at[:, :LANES], kvn[:, DIFF_WIDTH:])
        o_ref[0] = _diff_finish(acc_scr, l_scr, lam_ref[0], gain_ref[...], out_scale, n_new)


def diff_sample_attention(dq, dkv_new, cache, layer, page_table, lam, gain, lam_init):
    n_seq, n_new, w = dq.shape
    n_pages = page_table.shape[1]
    page = cache.shape[2]
    per = SAMPLE_PAGES_PER_STEP
    assert n_pages % per == 0 and n_new & (n_new - 1) == 0 and n_new % 8 == 0
    rows = N_DIFF_MAPS * n_new
    tk = per * page
    page_spec = lambda k: pl.BlockSpec((1, 1, page, 2 * w), lambda b, j, pt: (layer, pt[b, j * per + k], 0, 0))
    return pl.pallas_call(
        functools.partial(_diff_sample_kernel, out_scale=1.0 - lam_init, n_new=n_new),
        out_shape=jax.ShapeDtypeStruct((n_seq, n_new, w), F32),
        grid_spec=pltpu.PrefetchScalarGridSpec(
            num_scalar_prefetch=1,
            grid=(n_seq, n_pages // per),
            in_specs=[
                pl.BlockSpec(memory_space=pltpu.SMEM),
                pl.BlockSpec((1, n_new, w), lambda b, j, pt: (b, 0, 0)),
                pl.BlockSpec((1, n_new, 2 * w), lambda b, j, pt: (b, 0, 0)),
                pl.BlockSpec((1, w), lambda b, j, pt: (0, 0)),
            ] + [page_spec(k) for k in range(per)],
            out_specs=pl.BlockSpec((1, n_new, w), lambda b, j, pt: (b, 0, 0)),
            scratch_shapes=[pltpu.VMEM((rows, w), F32), pltpu.VMEM((rows, tk), F32), pltpu.VMEM((rows, tk), BF16),
                            pltpu.VMEM((rows, LANES), F32), pltpu.VMEM((rows, LANES), F32),
                            pltpu.VMEM((rows, LANES), F32), pltpu.VMEM((rows, w), F32),
                            pltpu.VMEM((tk, 2 * w), BF16)],
        ),
        compiler_params=pltpu.CompilerParams(
            dimension_semantics=("arbitrary", "arbitrary"), vmem_limit_bytes=VMEM_LIMIT_BYTES),
        name="diff_sample_attention",
    )(page_table, lam.reshape(1), dq, dkv_new, gain.reshape(1, w), *([cache] * per))


def _nsa_sample_kernel(pt_ref, q_ref, gate_ref, skvn_ref, wkvn_ref, win_ref,
                       pe_a_ref, pe_b_ref, w1a_ref, w1b_ref, w2_ref, *refs, n_new, n_pages, past_len):
    del pt_ref
    cmp_refs, slc_refs, o_ref = refs[:n_pages], refs[n_pages:2 * n_pages], refs[2 * n_pages]
    x_scr, ks_scr, kw_scr, q_scr, s_scr, p_scr, m_scr, l_scr, a_scr, acc_scr = refs[2 * n_pages + 1:]
    chunks = cmp_refs[0].shape[2]
    page = slc_refs[0].shape[2]
    n_cmp = n_pages * chunks
    n_keys = ks_scr.shape[0]
    n_win = win_ref.shape[2]
    tq = n_new
    nt = (((1,), (1,)), ((), ()))

    for p in range(n_pages):
        x_scr[p * chunks:(p + 1) * chunks, :] = cmp_refs[p][0, 0]
        ks_scr[p * page:(p + 1) * page, :] = slc_refs[p][0, 0].astype(BF16)
    ks_scr[n_pages * page:, :] = _pad_rows(skvn_ref[0], LANES).astype(BF16)
    kw_scr[:n_win, :] = win_ref[0, 0].astype(BF16)
    kw_scr[n_win:, :] = _pad_rows(wkvn_ref[0], LANES).astype(BF16)
    kvc = _compress_rows(x_scr[...], pe_a_ref, pe_b_ref, w1a_ref, w1b_ref, w2_ref).astype(BF16)

    lane = lax.broadcasted_iota(jnp.int32, (tq, LANES), 1)
    qidx = lax.broadcasted_iota(jnp.int32, (tq, LANES), 0)
    qpos = past_len + qidx
    gates = jax.nn.sigmoid(gate_ref[0])
    n_blk = -(-(past_len + n_new) // SLC_LEN)
    assert n_cmp == LANES and n_blk <= LANES
    erow = lax.broadcasted_iota(jnp.int32, (LANES, n_keys), 0)
    ecol = jnp.right_shift(lax.broadcasted_iota(jnp.int32, (LANES, n_keys), 1), SLC_SHIFT)
    expand = jnp.where(erow == ecol, 1.0, 0.0).astype(BF16)
    cblk = lax.broadcasted_iota(jnp.int32, (n_cmp, LANES), 0) * CMP_STRIDE
    sblk = lax.broadcasted_iota(jnp.int32, (n_cmp, LANES), 1) * SLC_LEN
    overlap = jnp.where((cblk < sblk + SLC_LEN) & (cblk + CMP_LEN > sblk), 1.0, 0.0).astype(BF16)

    def finish():
        l = jnp.maximum(jnp.sum(l_scr[...], axis=1, keepdims=True), 1e-30)
        return acc_scr[...] / l

    for kvh in range(NSA_KV_HEADS):
        half = slice(kvh * NSA_HD, (kvh + 1) * NSA_HD)
        for g in range(NSA_GROUP):
            h = kvh * NSA_GROUP + g
            qh = q_ref[0, :, h * NSA_HD:(h + 1) * NSA_HD] * (NSA_HD ** -0.5 * LOG2E)
            zero = jnp.zeros_like(qh)
            q_scr[g * tq:(g + 1) * tq, :] = jnp.concatenate([qh, zero] if kvh == 0 else [zero, qh], axis=1)
        qb = q_scr[...].astype(BF16)

        sc = lax.dot_general(qb, kvc[:, :KV_LANES], nt, preferred_element_type=F32)
        psum = jnp.zeros((tq, LANES), F32)
        pcs = []
        for g in range(NSA_GROUP):
            cmask = lane * CMP_STRIDE + (CMP_LEN - 1) <= qpos
            cmask = cmask & (lane < n_cmp - 1)
            t = jnp.where(cmask, sc[g * tq:(g + 1) * tq], -jnp.inf)
            mx = jnp.max(t, axis=1, keepdims=True)
            mx = jnp.where(mx == -jnp.inf, 0.0, mx)
            e = jnp.where(cmask, jnp.exp2(t - mx), 0.0)
            pc = e / jnp.maximum(jnp.sum(e, axis=1, keepdims=True), 1e-30)
            psum = psum + pc
            pcs.append(pc)
        o_cmp = jnp.dot(jnp.concatenate(pcs, axis=0).astype(BF16), kvc[:, KV_LANES:], preferred_element_type=F32)

        imp = sum(jnp.dot(t, overlap, preferred_element_type=F32) for t in _split3(psum))
        jt = jnp.right_shift(qpos, SLC_SHIFT)
        forced = (lane == 0) | (lane == jt) | (lane == jt - 1)
        score = jnp.where(forced, jnp.inf, jnp.where(lane <= jt, imp, -jnp.inf))
        cnt = jnp.zeros((tq, LANES), F32)
        for jp in range(n_blk):
            col = jnp.broadcast_to(score[:, jp:jp + 1], (tq, LANES))
            tie = jnp.where(jp < lane, 1.0, 0.0)
            cnt = cnt + jnp.where(col > score, 1.0, jnp.where(col == score, tie, 0.0))
        sel = jnp.where(cnt < float(N_SEL), 1.0, 0.0).astype(BF16)
        member = jnp.dot(sel, expand, preferred_element_type=F32)

        _reset_softmax(m_scr, l_scr, acc_scr)
        s_scr[...] = lax.dot_general(qb, ks_scr[:, :KV_LANES], nt, preferred_element_type=F32)

        def slc_mask(r, j):
            return (member[:, j * LANES:(j + 1) * LANES] > 0.5) & (j * LANES + lane <= qpos)

        _softmax_slabs(s_scr, p_scr, m_scr, l_scr, a_scr, NSA_GROUP, tq, n_keys, slc_mask)
        _accumulate(acc_scr, a_scr, p_scr, ks_scr[:, KV_LANES:])
        o_slc = finish()

        _reset_softmax(m_scr, l_scr, acc_scr)
        n_wkeys = kw_scr.shape[0]
        s_scr[:, :n_wkeys] = lax.dot_general(qb, kw_scr[:, :KV_LANES], nt, preferred_element_type=F32)

        def win_mask(r, j):
            kpos = past_len - n_win + j * LANES + lane
            return (kpos <= qpos) & (kpos > qpos - WINDOW)

        _softmax_slabs(s_scr, p_scr, m_scr, l_scr, a_scr, NSA_GROUP, tq, n_wkeys, win_mask)
        _accumulate(acc_scr, a_scr, p_scr.at[:, :n_wkeys], kw_scr[:, KV_LANES:])
        o_win = finish()

        for g in range(NSA_GROUP):
            rows = slice(g * tq, (g + 1) * tq)
            h = kvh * NSA_GROUP + g
            c = h * 3
            o_ref[0, :, h * NSA_HD:(h + 1) * NSA_HD] = (
                gates[:, c:c + 1] * o_cmp[rows, half] + gates[:, c + 1:c + 2] * o_slc[rows, half]
                + gates[:, c + 2:c + 3] * o_win[rows, half])


def nsa_sample_attention(nq, gate, skv_new, wkv_new, cmp_cache, slc_cache, win_state, layer, page_table,
                         cmp_weights):
    n_seq, n_new, w = nq.shape
    n_pages = page_table.shape[1]
    depth, pool, page, kvw = slc_cache.shape
    chunks = page // CMP_STRIDE
    n_win = win_state.shape[2]
    past_len = n_pages * page
    assert n_new % 8 == 0 and n_new <= LANES and n_win % LANES == 0
    pe_a, pe_b, w1a, w1b, w2 = cmp_weights
    cmp_chunks = cmp_cache.reshape(depth, pool, chunks, CMP_CHUNK_WIDTH)
    rows = NSA_GROUP * n_new
    n_keys = past_len + LANES
    seq = lambda b, pt: (b, 0, 0)
    const = lambda b, pt: (0, 0)
    cmp_spec = lambda p: pl.BlockSpec((1, 1, chunks, CMP_CHUNK_WIDTH), lambda b, pt: (layer, pt[b, p], 0, 0))
    slc_spec = lambda p: pl.BlockSpec((1, 1, page, kvw), lambda b, pt: (layer, pt[b, p], 0, 0))
    return pl.pallas_call(
        functools.partial(_nsa_sample_kernel, n_new=n_new, n_pages=n_pages, past_len=past_len),
        out_shape=jax.ShapeDtypeStruct((n_seq, n_new, w), F32),
        grid_spec=pltpu.PrefetchScalarGridSpec(
            num_scalar_prefetch=1,
            grid=(n_seq,),
            in_specs=[
                pl.BlockSpec((1, n_new, w), seq), pl.BlockSpec((1, n_new, gate.shape[2]), seq),
                pl.BlockSpec((1, n_new, kvw), seq), pl.BlockSpec((1, n_new, kvw), seq),
                pl.BlockSpec((1, 1, n_win, kvw), lambda b, pt: (layer, b, 0, 0)),
                pl.BlockSpec((1, CMP_CHUNK_WIDTH), const), pl.BlockSpec((1, CMP_CHUNK_WIDTH), const),
                pl.BlockSpec((CMP_CHUNK_WIDTH, kvw), const), pl.BlockSpec((CMP_CHUNK_WIDTH, kvw), const),
                pl.BlockSpec((kvw, kvw), const),
            ] + [cmp_spec(p) for p in range(n_pages)] + [slc_spec(p) for p in range(n_pages)],
            out_specs=pl.BlockSpec((1, n_new, w), seq),
            scratch_shapes=[pltpu.VMEM((n_pages * chunks, CMP_CHUNK_WIDTH), F32),
                            pltpu.VMEM((n_keys, kvw), BF16), pltpu.VMEM((n_win + LANES, kvw), BF16),
                            pltpu.VMEM((rows, KV_LANES), F32), pltpu.VMEM((rows, n_keys), F32),
                            pltpu.VMEM((rows, n_keys), F32), pltpu.VMEM((rows, LANES), F32),
                            pltpu.VMEM((rows, LANES), F32), pltpu.VMEM((rows, LANES), F32),
                            pltpu.VMEM((rows, KV_LANES), F32)],
        ),
        compiler_params=pltpu.CompilerParams(
            dimension_semantics=("arbitrary",), vmem_limit_bytes=VMEM_LIMIT_BYTES),
        name="nsa_sample_attention",
    )(page_table, nq, gate, skv_new, wkv_new, win_state, pe_a, pe_b, w1a, w1b, w2,
      *([cmp_chunks] * n_pages), *([slc_cache] * n_pages))


SSM_LANES = SSM_GROUPS * SSM_STATE
SSM_STEPS = 256


def _ssm_kernel(u_ref, h0_ref, a_ref, bb_ref, cc_ref, d_ref, glu_ref, o_ref, hlast_ref, h_scr, st_scr,
                *, n_seq, n_steps, seq_major):
    @pl.when(pl.program_id(0) == 0)
    def _():
        st_scr[...] = h0_ref[...]

    u = u_ref[...].reshape(n_seq * n_steps, SSM_WIDTH)
    h_scr[...] = jnp.dot(u.astype(BF16), bb_ref[...], preferred_element_type=F32)
    slab = 1 if seq_major else n_seq
    a_re = jnp.broadcast_to(a_ref[:, :SSM_LANES], (slab, SSM_LANES))
    a_im = jnp.broadcast_to(a_ref[:, SSM_LANES:], (slab, SSM_LANES))

    def advance(rows, h_re, h_im):
        bu = h_scr[rows, :]
        n_re = a_re * h_re - a_im * h_im + bu[:, :SSM_LANES]
        n_im = a_re * h_im + a_im * h_re + bu[:, SSM_LANES:]
        h_scr[rows, :SSM_LANES] = n_re
        h_scr[rows, SSM_LANES:] = n_im
        return n_re, n_im

    if seq_major:
        def body(t, carry):
            return tuple(advance(pl.ds(s * n_steps + t, 1), *carry[s]) for s in range(n_seq))
        init = tuple((st_scr[s:s + 1, :SSM_LANES], st_scr[s:s + 1, SSM_LANES:]) for s in range(n_seq))
        last = lax.fori_loop(0, n_steps, body, init)
        for s in range(n_seq):
            st_scr[s:s + 1, :SSM_LANES] = last[s][0]
            st_scr[s:s + 1, SSM_LANES:] = last[s][1]
    else:
        h_re, h_im = st_scr[:, :SSM_LANES], st_scr[:, SSM_LANES:]
        for t in range(n_steps):
            h_re, h_im = advance(slice(t * n_seq, (t + 1) * n_seq), h_re, h_im)
        st_scr[:, :SSM_LANES] = h_re
        st_scr[:, SSM_LANES:] = h_im
    hlast_ref[...] = st_scr[...]

    y = jnp.dot(h_scr[...].astype(BF16), cc_ref[...], preferred_element_type=F32) + d_ref[...] * u
    z = jnp.dot(y.astype(BF16), glu_ref[...], preferred_element_type=F32)
    o_ref[...] = (z[:, :SSM_WIDTH] * jax.nn.sigmoid(z[:, SSM_WIDTH:])).reshape(o_ref.shape)


def _ssm_weights(disc, c_re, c_im, d, glu_w):
    a_re, a_im, bb_re, bb_im = disc
    eye = jnp.eye(SSM_GROUPS, dtype=F32)
    b_in = jnp.concatenate([jnp.einsum('gpc,gx->gcxp', bb, eye).reshape(SSM_WIDTH, SSM_LANES)
                            for bb in (bb_re, bb_im)], axis=1).astype(BF16)
    c_out = jnp.concatenate([jnp.einsum('gcp,gx->gpxc', cc, eye).reshape(SSM_LANES, SSM_WIDTH)
                             for cc in (c_re.astype(F32), -c_im.astype(F32))], axis=0).astype(BF16)
    a = jnp.concatenate([a_re.reshape(1, SSM_LANES), a_im.reshape(1, SSM_LANES)], axis=1)
    return a, b_in, c_out, d.reshape(1, SSM_WIDTH).astype(F32), glu_w.astype(BF16)


def ssm_group(u, h0, weights):
    n_seq, t, w = u.shape
    a, b_in, c_out, d, glu = weights
    seq_major = n_seq % 8 != 0
    if seq_major:
        steps = min(SSM_STEPS, t)
        x, block, imap = u, (n_seq, steps, w), (lambda i: (0, i, 0))
    else:
        steps = t
        x, block, imap = jnp.transpose(u, (1, 0, 2)).reshape(t * n_seq, w), (t * n_seq, w), (lambda i: (0, 0))
    assert t % steps == 0
    rows = steps * n_seq
    const = lambda i: (0, 0)
    out, h_last = pl.pallas_call(
        functools.partial(_ssm_kernel, n_seq=n_seq, n_steps=steps, seq_major=seq_major),
        out_shape=(jax.ShapeDtypeStruct(x.shape, F32), jax.ShapeDtypeStruct((n_seq, 2 * SSM_LANES), F32)),
        grid=(t // steps,),
        in_specs=[
            pl.BlockSpec(block, imap),
            pl.BlockSpec((n_seq, 2 * SSM_LANES), const),
            pl.BlockSpec((1, 2 * SSM_LANES), const),
            pl.BlockSpec((SSM_WIDTH, 2 * SSM_LANES), const),
            pl.BlockSpec((2 * SSM_LANES, SSM_WIDTH), const),
            pl.BlockSpec((1, SSM_WIDTH), const),
            pl.BlockSpec((SSM_WIDTH, 2 * SSM_WIDTH), const),
        ],
        out_specs=(pl.BlockSpec(block, imap), pl.BlockSpec((n_seq, 2 * SSM_LANES), const)),
        scratch_shapes=[pltpu.VMEM((rows, 2 * SSM_LANES), F32), pltpu.VMEM((n_seq, 2 * SSM_LANES), F32)],
        compiler_params=pltpu.CompilerParams(
            dimension_semantics=("arbitrary",), vmem_limit_bytes=VMEM_LIMIT_BYTES),
        name="ssm_mix",
    )(x, h0, a, b_in, c_out, d, glu)
    if not seq_major:
        out = jnp.transpose(out.reshape(t, n_seq, w), (1, 0, 2))
    return out, h_last.reshape(n_seq, 2, SSM_GROUPS, SSM_STATE)


GATHER_PAGES_PER_STEP = 4


def _gather_pages_kernel(pt_ref, *refs):
    del pt_ref
    o_ref = refs[-1]
    page = refs[0].shape[1]
    for k, src in enumerate(refs[:-1]):
        o_ref[0, k * page:(k + 1) * page, :] = src[0]


def gather_pages(cache, page_table):
    n_seq, n_pages = page_table.shape
    page = cache.shape[1]
    width = math.prod(cache.shape[2:])
    per = GATHER_PAGES_PER_STEP
    assert n_pages % per == 0
    page_spec = lambda k: pl.BlockSpec((1, page, width), lambda b, j, pt: (pt[b, j * per + k], 0, 0))
    flat = cache.reshape(cache.shape[0], page, width)
    out = pl.pallas_call(
        _gather_pages_kernel,
        out_shape=jax.ShapeDtypeStruct((n_seq, n_pages * page, width), cache.dtype),
        grid_spec=pltpu.PrefetchScalarGridSpec(
            num_scalar_prefetch=1,
            grid=(n_seq, n_pages // per),
            in_specs=[page_spec(k) for k in range(per)],
            out_specs=pl.BlockSpec((1, per * page, width), lambda b, j, pt: (b, j, 0)),
        ),
        compiler_params=pltpu.CompilerParams(
            dimension_semantics=("arbitrary", "arbitrary"), vmem_limit_bytes=VMEM_LIMIT_BYTES),
        name="gather_pages",
    )(page_table, *([flat] * per))
    return out.reshape((n_seq, n_pages * page) + cache.shape[2:])


def rmsnorm(x, g):
    xf = x.astype(jnp.float32)
    y = xf * lax.rsqrt(jnp.mean(xf * xf, axis=-1, keepdims=True) + EPS)
    return (y * g.astype(jnp.float32)).astype(x.dtype)


def masked_softmax(s, mask):
    s = jnp.where(mask, s.astype(jnp.float32), -jnp.inf)
    m = jnp.max(s, axis=-1, keepdims=True)
    m = jnp.where(jnp.isfinite(m), m, 0.0)
    e = jnp.where(mask, jnp.exp(s - m), 0.0)
    return e / jnp.maximum(jnp.sum(e, axis=-1, keepdims=True), 1e-30)


def mixer_projection(x, norm_g, w_in):
    b, t, _ = x.shape
    sizes = (DIFF_HEADS * 2 * DIFF_QK, DIFF_HEADS * 2 * DIFF_QK, DIFF_WIDTH, NSA_WIDTH,
             NSA_KV_WIDTH, NSA_KV_WIDTH, NSA_KV_WIDTH, 3 * NSA_HEADS, SSM_WIDTH)
    offsets = np.cumsum(sizes)[:-1].tolist()
    dq, dk, dv, nq, ckv, skv, wkv, gate, u = jnp.split(rmsnorm(x, norm_g) @ w_in, offsets, axis=-1)
    kv5 = lambda z: z.reshape(b, t, 2, NSA_KV_HEADS, NSA_HD)
    dq = dq.reshape(b, t, DIFF_HEADS, 2, DIFF_QK)
    dkv = jnp.stack([dk.reshape(b, t, DIFF_HEADS, 2 * DIFF_QK), dv.reshape(b, t, DIFF_HEADS, DIFF_V)], axis=2)
    return dq, dkv, nq.reshape(b, t, NSA_HEADS, NSA_HD), kv5(ckv), kv5(skv), kv5(wkv), gate, u


def diff_core(q, k, v, qpos, kpos, lam):
    s = jnp.einsum('bqhcd,bkhcd->bhcqk', q, k) * (DIFF_QK ** -0.5)
    p = masked_softmax(s, kpos[None, :] <= qpos[:, None])
    a = p[:, :, 0] - lam * p[:, :, 1]
    return jnp.einsum('bhqk,bkhd->bqhd', a.astype(v.dtype), v)


def diff_prompt(dq, dkv, lam):
    b, t = dq.shape[:2]
    k = dkv[:, :, 0].reshape(b, t, DIFF_HEADS, 2, DIFF_QK)
    v = dkv[:, :, 1]
    kpos = jnp.arange(t)

    def block(i):
        s0 = i * Q_BLOCK
        qb = lax.dynamic_slice_in_dim(dq, s0, Q_BLOCK, axis=1)
        return diff_core(qb, k, v, s0 + jnp.arange(Q_BLOCK), kpos, lam)

    o = lax.map(block, jnp.arange(t // Q_BLOCK))
    return jnp.moveaxis(o, 0, 1).reshape(b, t, DIFF_HEADS, DIFF_V)


def diff_sample(dq, dkv, past_dkv, lam):
    b, s = dq.shape[:2]
    past_len = past_dkv.shape[1]
    kv = jnp.concatenate([past_dkv, dkv], axis=1)
    t = kv.shape[1]
    k = kv[:, :, 0].reshape(b, t, DIFF_HEADS, 2, DIFF_QK)
    return diff_core(dq, k, kv[:, :, 1], past_len + jnp.arange(s), jnp.arange(t), lam)


def diff_finish(o, gain, lam_init):
    b, t = o.shape[:2]
    return (rmsnorm(o, gain.reshape(DIFF_HEADS, DIFF_V)) * (1.0 - lam_init)).reshape(b, t, DIFF_WIDTH)


def nsa_compress(kv, pe, w1, w2):
    b, t = kv.shape[:2]
    r = CMP_LEN // CMP_STRIDE
    ns = t // CMP_STRIDE
    nc = ns - r + 1
    ch = kv[:, :ns * CMP_STRIDE].reshape(b, ns, CMP_STRIDE, 2, NSA_KV_HEADS, NSA_HD)
    blk = jnp.concatenate([ch[:, i:i + nc] for i in range(r)], axis=2)
    blk = blk + jnp.transpose(pe, (1, 0, 2))[:, :, None, :]
    flat = jnp.transpose(blk, (0, 1, 3, 4, 2, 5)).reshape(b, nc, 2, NSA_KV_HEADS, CMP_LEN * NSA_HD)
    hid = jax.nn.gelu(jnp.einsum('bnckf,cfe->bncke', flat, w1))
    out = jnp.einsum('bncke,ced->bnckd', hid, w2)
    cend = jnp.arange(nc) * CMP_STRIDE + CMP_LEN - 1
    return out[:, :, 0], out[:, :, 1], cend


def slc_blocks(kv):
    b, t = kv.shape[:2]
    nb = -(-t // SLC_LEN)
    kv = jnp.pad(kv, ((0, 0), (0, nb * SLC_LEN - t), (0, 0), (0, 0), (0, 0)))
    blk = jnp.transpose(kv.reshape(b, nb, SLC_LEN, 2, NSA_KV_HEADS, NSA_HD), (3, 0, 4, 1, 2, 5))
    return blk[0], blk[1]


def overlap_matrix(nc, nb):
    cs = jnp.arange(nc) * CMP_STRIDE
    js = jnp.arange(nb) * SLC_LEN
    return ((cs[:, None] < js[None, :] + SLC_LEN) & (cs[:, None] + CMP_LEN > js[None, :])).astype(jnp.float32)


def nsa_core(q, qpos, kc, vc, cend, ks, vs, kw, vw, kwpos, gate):
    b, nq = q.shape[:2]
    qg = q.reshape(b, nq, NSA_KV_HEADS, NSA_GROUP, NSA_HD)
    scale = NSA_HD ** -0.5
    s = jnp.einsum('bqkgd,bnkd->bqkgn', qg, kc) * scale
    p_c = masked_softmax(s, (cend[None, :] <= qpos[:, None])[None, :, None, None, :])
    o_c = jnp.einsum('bqkgn,bnkd->bqkgd', p_c.astype(vc.dtype), vc)
    nb = ks.shape[2]
    imp = jnp.einsum('bqkn,nj->bqkj', jnp.sum(p_c, axis=3), overlap_matrix(kc.shape[1], nb))
    j = jnp.arange(nb)[None, :]
    jt = (qpos // SLC_LEN)[:, None]
    forced = ((j == 0) | (j == jt) | (j == jt - 1))[None, :, None, :]
    score = jnp.where(forced, jnp.inf, jnp.where((j <= jt)[None, :, None, :], imp, -jnp.inf))
    n_sel = min(N_SEL, nb)
    _, idx = lax.top_k(score, n_sel)
    bi = jnp.arange(b)[:, None, None, None]
    hi = jnp.arange(NSA_KV_HEADS)[None, None, :, None]
    ksel = ks[bi, hi, idx]
    vsel = vs[bi, hi, idx]
    s = jnp.einsum('bqkgd,bqknld->bqkgnl', qg, ksel) * scale
    kpos = idx[..., None] * SLC_LEN + jnp.arange(SLC_LEN)
    smask = (kpos <= qpos[None, :, None, None, None]).reshape(b, nq, NSA_KV_HEADS, 1, n_sel * SLC_LEN)
    p_s = masked_softmax(s.reshape(b, nq, NSA_KV_HEADS, NSA_GROUP, n_sel * SLC_LEN), smask).reshape(s.shape)
    o_s = jnp.einsum('bqkgnl,bqknld->bqkgd', p_s.astype(vsel.dtype), vsel)
    s = jnp.einsum('bqkgd,bwkd->bqkgw', qg, kw) * scale
    wmask = (kwpos[None, :] <= qpos[:, None]) & (kwpos[None, :] > qpos[:, None] - WINDOW) & (kwpos[None, :] >= 0)
    p_w = masked_softmax(s, wmask[None, :, None, None, :])
    o_w = jnp.einsum('bqkgw,bwkd->bqkgd', p_w.astype(vw.dtype), vw)
    g = jax.nn.sigmoid(gate.astype(jnp.float32)).reshape(b, nq, NSA_KV_HEADS, NSA_GROUP, 3)
    o = g[..., 0:1] * o_c + g[..., 1:2] * o_s + g[..., 2:3] * o_w
    return o.reshape(b, nq, NSA_WIDTH).astype(q.dtype)


def nsa_prompt(nq, ckv, skv, wkv, gate, pe, w1, w2):
    b, t = nq.shape[:2]
    kc, vc, cend = nsa_compress(ckv, pe, w1, w2)
    ks, vs = slc_blocks(skv)
    wpad = jnp.pad(wkv, ((0, 0), (WINDOW, 0), (0, 0), (0, 0), (0, 0)))

    def block(i):
        s0 = i * Q_BLOCK
        qb = lax.dynamic_slice_in_dim(nq, s0, Q_BLOCK, axis=1)
        gb = lax.dynamic_slice_in_dim(gate, s0, Q_BLOCK, axis=1)
        wb = lax.dynamic_slice_in_dim(wpad, s0, WINDOW + Q_BLOCK, axis=1)
        qpos = s0 + jnp.arange(Q_BLOCK)
        kwpos = s0 - WINDOW + jnp.arange(WINDOW + Q_BLOCK)
        return nsa_core(qb, qpos, kc, vc, cend, ks, vs, wb[:, :, 0], wb[:, :, 1], kwpos, gb)

    o = lax.map(block, jnp.arange(t // Q_BLOCK))
    return jnp.moveaxis(o, 0, 1).reshape(b, t, NSA_WIDTH), wkv[:, -min(WINDOW, t):]


def nsa_sample(nq, ckv, skv, wkv, gate, past_ckv, past_skv, win_buf, past_len, pe, w1, w2):
    s = nq.shape[1]
    wb = win_buf.shape[1]
    kc, vc, cend = nsa_compress(jnp.concatenate([past_ckv, ckv], axis=1), pe, w1, w2)
    ks, vs = slc_blocks(jnp.concatenate([past_skv, skv], axis=1))
    wfull = jnp.concatenate([win_buf, wkv], axis=1)
    qpos = past_len + jnp.arange(s)
    kwpos = past_len - wb + jnp.arange(wb + s)
    o = nsa_core(nq, qpos, kc, vc, cend, ks, vs, wfull[:, :, 0], wfull[:, :, 1], kwpos, gate)
    return o, wfull[:, -min(WINDOW, wb + s):]


def ssm_discretize(lam_re, lam_im, log_dt, b_re, b_im):
    f32 = jnp.float32
    lr = jnp.minimum(lam_re.astype(f32), -1e-4)
    li = lam_im.astype(f32)
    dt = jnp.exp(log_dt.astype(f32))[:, None]
    mag = jnp.exp(lr * dt)
    a_re, a_im = mag * jnp.cos(li * dt), mag * jnp.sin(li * dt)
    den = lr * lr + li * li
    nr = a_re - 1.0
    f_re = (nr * lr + a_im * li) / den
    f_im = (a_im * lr - nr * li) / den
    b_re, b_im = b_re.astype(f32), b_im.astype(f32)
    bb_re = f_re[..., None] * b_re - f_im[..., None] * b_im
    bb_im = f_re[..., None] * b_im + f_im[..., None] * b_re
    return a_re, a_im, bb_re, bb_im


def ssm_combine(e1, e2):
    a1r, a1i, b1r, b1i = e1
    a2r, a2i, b2r, b2i = e2
    return (a2r * a1r - a2i * a1i, a2r * a1i + a2i * a1r,
            a2r * b1r - a2i * b1i + b2r, a2r * b1i + a2i * b1r + b2i)


def ssm_mixer(u, h0_re, h0_im, disc, c_re, c_im, d, glu_w):
    f32 = jnp.float32
    a_re, a_im, bb_re, bb_im = disc
    b, t, _ = u.shape
    uf = u.astype(f32)
    ug = uf.reshape(b, t, SSM_GROUPS, SSM_CH)
    bu_re = jnp.einsum('gpc,btgc->btgp', bb_re, ug)
    bu_im = jnp.einsum('gpc,btgc->btgp', bb_im, ug)
    h0_re, h0_im = h0_re.astype(f32), h0_im.astype(f32)
    bu_re = bu_re.at[:, 0].add(a_re * h0_re - a_im * h0_im)
    bu_im = bu_im.at[:, 0].add(a_re * h0_im + a_im * h0_re)
    ar = jnp.broadcast_to(a_re, bu_re.shape)
    ai = jnp.broadcast_to(a_im, bu_im.shape)
    _, _, h_re, h_im = lax.associative_scan(ssm_combine, (ar, ai, bu_re, bu_im), axis=1)
    y = jnp.einsum('gcp,btgp->btgc', c_re.astype(f32), h_re) - jnp.einsum('gcp,btgp->btgc', c_im.astype(f32), h_im)
    y = y.reshape(b, t, SSM_WIDTH) + d.astype(f32) * uf
    z = y @ glu_w.astype(f32)
    out = z[..., :SSM_WIDTH] * jax.nn.sigmoid(z[..., SSM_WIDTH:])
    return out.astype(u.dtype), jnp.stack([h_re[:, -1], h_im[:, -1]], axis=1).astype(u.dtype)


def kernel(x_prompt, x_sample, cache_diff_kv, cache_nsa_cmp_kv, cache_nsa_slc_kv, state_nsa_win_kv, state_ssm, page_table, ffn1_norm, ffn1_w_gate, ffn1_w_up, ffn1_w_down, mix_norm, w_in, diff_lambda, diff_head_norm, nsa_cmp_pe, nsa_cmp_w1, nsa_cmp_w2, ssm_lambda_re, ssm_lambda_im, ssm_log_dt, ssm_b_re, ssm_b_im, ssm_c_re, ssm_c_im, ssm_d, ssm_glu_w, w_out, ffn2_norm, ffn2_w_gate, ffn2_w_up, ffn2_w_down, final_norm):
    f32 = jnp.float32
    n_pages = page_table.shape[1]
    past_len = n_pages * cache_diff_kv.shape[2]
    b_p, t_p, d = x_prompt.shape
    b_s, t_s, _ = x_sample.shape
    n_p = b_p * t_p
    x = jnp.concatenate([x_prompt.reshape(n_p, d), x_sample.reshape(b_s * t_s, d)], axis=0)
    n_pool, page = cache_diff_kv.shape[1:3]
    n_win = state_nsa_win_kv.shape[2]
    cache_diff = cache_diff_kv.reshape(DEPTH, n_pool, page, 2 * DIFF_WIDTH)
    cache_cmp = cache_nsa_cmp_kv.reshape(DEPTH, n_pool, page, NSA_KV_WIDTH)
    cache_slc = cache_nsa_slc_kv.reshape(DEPTH, n_pool, page, NSA_KV_WIDTH)
    win_state = state_nsa_win_kv.reshape(DEPTH, b_s, n_win, NSA_KV_WIDTH)
    dkv_p, dkv_s, ckv_p, ckv_s, skv_p, skv_s, win_p, win_s, ssm_p, ssm_s = ([] for _ in range(10))
    for l in range(DEPTH):
        lam_init = 0.8 - 0.6 * math.exp(-0.3 * l)
        lv = diff_lambda[l].astype(f32)
        lam = jnp.exp(jnp.sum(lv[0] * lv[1])) - jnp.exp(jnp.sum(lv[2] * lv[3])) + lam_init
        disc = ssm_discretize(ssm_lambda_re[l], ssm_lambda_im[l], ssm_log_dt[l], ssm_b_re[l], ssm_b_im[l])
        ssm_w = (ssm_c_re[l], ssm_c_im[l], ssm_d[l], ssm_glu_w[l])
        cmp_w = (nsa_cmp_pe[l], nsa_cmp_w1[l], nsa_cmp_w2[l])

        x = ffn_half(x, ffn1_norm[l], ffn1_w_gate[l], ffn1_w_up[l], ffn1_w_down[l])
        xp = x[:n_p].reshape(b_p, t_p, d)
        xs = x[n_p:].reshape(b_s, t_s, d)

        dq, dkv, nq, ckv, skv, wkv, gate, u = mixer_projection(xp, mix_norm[l], w_in[l])
        o_d = diff_prompt_attention(dq.reshape(b_p, t_p, DIFF_WIDTH), dkv[:, :, 0].reshape(b_p, t_p, DIFF_WIDTH),
                                    dkv[:, :, 1].reshape(b_p, t_p, DIFF_WIDTH), lam, diff_head_norm[l], lam_init)
        cmp_big = _compress_weights(*cmp_w)
        kvc = nsa_compress_all(ckv.reshape(b_p, t_p, NSA_KV_WIDTH), cmp_big, t_p // CMP_STRIDE)
        o_n = nsa_prompt_attention(nq.reshape(b_p, t_p, NSA_WIDTH), gate, kvc,
                                   skv.reshape(b_p, t_p, NSA_KV_WIDTH).astype(BF16),
                                   wkv.reshape(b_p, t_p, NSA_KV_WIDTH).astype(BF16))
        win = wkv[:, -min(WINDOW, t_p):]
        ssm_big = _ssm_weights(disc, *ssm_w)
        o_s, h = ssm_group(u, jnp.zeros((b_p, 2 * SSM_LANES), f32), ssm_big)
        xp = xp + jnp.concatenate([o_d, o_n, o_s], axis=-1) @ w_out[l]
        dkv_p.append(dkv); ckv_p.append(ckv); skv_p.append(skv); win_p.append(win); ssm_p.append(h)

        dq, dkv, nq, ckv, skv, wkv, gate, u = mixer_projection(xs, mix_norm[l], w_in[l])
        o_d = diff_sample_attention(dq.reshape(b_s, t_s, DIFF_WIDTH), dkv.reshape(b_s, t_s, 2 * DIFF_WIDTH),
                                    cache_diff, l, page_table, lam, diff_head_norm[l], lam_init)
        o_n = nsa_sample_attention(nq.reshape(b_s, t_s, NSA_WIDTH), gate, skv.reshape(b_s, t_s, NSA_KV_WIDTH),
                                   wkv.reshape(b_s, t_s, NSA_KV_WIDTH), cache_cmp, cache_slc, win_state, l,
                                   page_table, cmp_big)
        win = jnp.concatenate([state_nsa_win_kv[l], wkv], axis=1)[:, -min(WINDOW, n_win + t_s):]
        o_s, h = ssm_group(u, state_ssm[l].reshape(b_s, 2 * SSM_LANES), ssm_big)
        xs = xs + jnp.concatenate([o_d, o_n, o_s], axis=-1) @ w_out[l]
        dkv_s.append(dkv); ckv_s.append(ckv); skv_s.append(skv); win_s.append(win); ssm_s.append(h)

        x = jnp.concatenate([xp.reshape(n_p, d), xs.reshape(b_s * t_s, d)], axis=0)
        x = ffn_half(x, ffn2_norm[l], ffn2_w_gate[l], ffn2_w_up[l], ffn2_w_down[l])

    y = rmsnorm(x, final_norm)
    y_prompt = y[:n_p].reshape(b_p, t_p, d)
    y_sample = y[n_p:].reshape(b_s, t_s, d)
    return (y_prompt, y_sample, jnp.stack(dkv_p), jnp.stack(dkv_s), jnp.stack(ckv_p), jnp.stack(ckv_s),
            jnp.stack(skv_p), jnp.stack(skv_s), jnp.stack(win_p), jnp.stack(win_s), jnp.stack(ssm_p), jnp.stack(ssm_s))
```
